```python
import math
import jax, jax.numpy as jnp
from jax import lax
import numpy as np

D_MODEL = 1024
BATCH = 4
SEQ = 4096
DEPTH = 4

N_A_LAYERS = DEPTH // 2
N_B_LAYERS = DEPTH - N_A_LAYERS
CONV_WIDTH = 31
DIFF_HEADS = 8
DIFF_HEAD_DIM = 64
DIFF_V_DIM = 2 * DIFF_HEAD_DIM
ROT_DIM = DIFF_HEAD_DIM // 4
ROPE_THETA = 500000.0
Q_BLOCK = 128
MEM_TOKENS = 256
MEM_HEADS = 4
MEM_HEAD_DIM = D_MODEL // MEM_HEADS
FFN_HIDDEN = -(-(8 * D_MODEL) // (3 * 256)) * 256
RMS_EPS = 1e-6
LN_EPS = 1e-5
SUBLN_EPS = 1e-5

kernel_name = "yoco_conformer_conv_diff_attn_memory_trunk"


def rms_norm(x, g, eps=RMS_EPS):
    xf = x.astype(jnp.float32)
    y = xf * lax.rsqrt(jnp.mean(xf * xf, axis=-1, keepdims=True) + eps)
    return (y * g.astype(jnp.float32)).astype(x.dtype)


def layer_norm(x, g, b, eps=LN_EPS):
    xf = x.astype(jnp.float32)
    mu = jnp.mean(xf, axis=-1, keepdims=True)
    xc = xf - mu
    y = xc * lax.rsqrt(jnp.mean(xc * xc, axis=-1, keepdims=True) + eps)
    return (y * g.astype(jnp.float32) + b.astype(jnp.float32)).astype(x.dtype)


def rope_tables(positions):
    inv_freq = ROPE_THETA ** (-jnp.arange(0, ROT_DIM, 2, dtype=jnp.float32) / ROT_DIM)
    ang = positions.astype(jnp.float32)[..., None] * inv_freq
    return jnp.cos(ang), jnp.sin(ang)


def apply_partial_rope(t, cos, sin):
    half = ROT_DIM // 2
    tf = t.astype(jnp.float32)
    r1 = tf[..., :half]
    r2 = tf[..., half:ROT_DIM]
    c = cos[:, :, None, None, :]
    s = sin[:, :, None, None, :]
    out = jnp.concatenate([r1 * c - r2 * s, r1 * s + r2 * c, tf[..., ROT_DIM:]], axis=-1)
    return out.astype(t.dtype)


def conformer_conv(h, w_pw1, b_pw1, w_dw, b_dw, ln_g, ln_b, w_pw2, b_pw2):
    u = h @ w_pw1 + b_pw1
    a, gate = jnp.split(u, 2, axis=-1)
    u = a * jax.nn.sigmoid(gate)
    u = lax.conv_general_dilated(
        u, w_dw[:, None, :].astype(u.dtype), window_strides=(1,),
        padding=((CONV_WIDTH - 1, 0),),
        dimension_numbers=("NWC", "WIO", "NWC"),
        feature_group_count=D_MODEL) + b_dw
    u = jax.nn.silu(layer_norm(u, ln_g, ln_b))
    return u @ w_pw2 + b_pw2


def shared_kv(x, kv_norm, w_k, w_v, cos, sin):
    B, S, _ = x.shape
    h = rms_norm(x, kv_norm)
    k = apply_partial_rope((h @ w_k).reshape(B, S, DIFF_HEADS, 2, DIFF_HEAD_DIM), cos, sin)
    v = (h @ w_v).reshape(B, S, DIFF_HEADS, DIFF_V_DIM)
    return k, v


def diff_attention(h, k, v, cos, sin, w_q, lq1, lk1, lq2, lk2, subln_g, w_o, lambda_init):
    B, S, _ = h.shape
    q = apply_partial_rope((h @ w_q).reshape(B, S, DIFF_HEADS, 2, DIFF_HEAD_DIM), cos, sin)
    lam = (jnp.exp(jnp.sum(lq1.astype(jnp.float32) * lk1.astype(jnp.float32)))
           - jnp.exp(jnp.sum(lq2.astype(jnp.float32) * lk2.astype(jnp.float32)))
           + lambda_init)
    n_blocks = S // Q_BLOCK
    q_blocks = q.reshape(B, n_blocks, Q_BLOCK, DIFF_HEADS, 2, DIFF_HEAD_DIM).swapaxes(0, 1)
    k_pos = jnp.arange(S)
    scale = DIFF_HEAD_DIM ** -0.5

    def one_block(args):
        blk, qb = args
        q_pos = blk * Q_BLOCK + jnp.arange(Q_BLOCK)
        s = jnp.einsum('bqhcd,bkhcd->bhcqk', qb, k,
                       preferred_element_type=jnp.float32) * scale
        mask = k_pos[None, :] <= q_pos[:, None]
        p = jax.nn.softmax(jnp.where(mask, s, -jnp.inf), axis=-1)
        a = p[:, :, 0] - lam * p[:, :, 1]
        return jnp.einsum('bhqk,bkhe->bqhe', a.astype(v.dtype), v)

    o = lax.map(one_block, (jnp.arange(n_blocks), q_blocks))
    o = o.swapaxes(0, 1).reshape(B, S, DIFF_HEADS, DIFF_V_DIM)
    o = rms_norm(o, subln_g, SUBLN_EPS) * (1.0 - lambda_init)
    return o.reshape(B, S, DIFF_HEADS * DIFF_V_DIM) @ w_o


def memory_cross_attention(h, mem, w_q, w_k, w_v, w_o):
    B, S, _ = h.shape
    M = mem.shape[1]
    q = (h @ w_q).reshape(B, S, MEM_HEADS, MEM_HEAD_DIM)
    k = (mem @ w_k).reshape(B, M, MEM_HEADS, MEM_HEAD_DIM)
    v = (mem @ w_v).reshape(B, M, MEM_HEADS, MEM_HEAD_DIM)
    s = jnp.einsum('bshd,bmhd->bhsm', q, k, preferred_element_type=jnp.float32) * MEM_HEAD_DIM ** -0.5
    p = jax.nn.softmax(s, axis=-1)
    o = jnp.einsum('bhsm,bmhd->bshd', p.astype(v.dtype), v).reshape(B, S, D_MODEL)
    return o @ w_o


def swiglu(h, w_gate, w_up, w_down):
    return (jax.nn.silu(h @ w_gate) * (h @ w_up)) @ w_down


def setup_inputs(seed: int = 0) -> dict:
    key = jax.random.key(seed)
    ks = iter(jax.random.split(key, 40))
    f32 = jnp.float32

    def w(shape, fan_in):
        return jax.random.normal(next(ks), shape, f32) * (fan_in ** -0.5)

    def gain(shape):
        return 1.0 + 0.02 * jax.random.normal(next(ks), shape, f32)

    def small(shape, scale=0.01):
        return scale * jax.random.normal(next(ks), shape, f32)

    D, F, A, Bn = D_MODEL, FFN_HIDDEN, N_A_LAYERS, N_B_LAYERS
    QK = DIFF_HEADS * 2 * DIFF_HEAD_DIM
    VW = DIFF_HEADS * DIFF_V_DIM
    x = jax.random.normal(next(ks), (BATCH, SEQ, D), f32)
    mem = jax.random.normal(next(ks), (BATCH, MEM_TOKENS, D), f32)
    offset = jax.random.randint(next(ks), (BATCH, 1), 0, 1024, dtype=jnp.int32)
    positions = (offset + jnp.arange(SEQ, dtype=jnp.int32)[None, :]).astype(jnp.int32)
    return {
        "x": x, "mem": mem, "positions": positions,
        "norm_mix": gain((DEPTH, D)), "norm_mem": gain((DEPTH, D)),
        "norm_ffn": gain((DEPTH, D)), "norm_final": gain((D,)),
        "conv_w_pw1": w((A, D, 2 * D), D), "conv_b_pw1": small((A, 2 * D)),
        "conv_w_dw": w((A, CONV_WIDTH, D), CONV_WIDTH), "conv_b_dw": small((A, D)),
        "conv_ln_g": gain((A, D)), "conv_ln_b": small((A, D)),
        "conv_w_pw2": w((A, D, D), D), "conv_b_pw2": small((A, D)),
        "kv_norm": gain((D,)), "w_k_shared": w((D, QK), D), "w_v_shared": w((D, VW), D),
        "diff_w_q": w((Bn, D, QK), D),
        "diff_lambda_q1": small((Bn, DIFF_HEAD_DIM), 0.1), "diff_lambda_k1": small((Bn, DIFF_HEAD_DIM), 0.1),
        "diff_lambda_q2": small((Bn, DIFF_HEAD_DIM), 0.1), "diff_lambda_k2": small((Bn, DIFF_HEAD_DIM), 0.1),
        "diff_subln_g": gain((Bn, DIFF_V_DIM)), "diff_w_o": w((Bn, VW, D), VW),
        "mem_w_q": w((DEPTH, D, D), D), "mem_w_k": w((DEPTH, D, D), D),
        "mem_w_v": w((DEPTH, D, D), D), "mem_w_o": w((DEPTH, D, D), D),
        "ffn_w_gate": w((DEPTH, D, F), D), "ffn_w_up": w((DEPTH, D, F), D),
        "ffn_w_down": w((DEPTH, F, D), F),
    }


def reference(x, mem, positions, norm_mix, norm_mem, norm_ffn, norm_final,
              conv_w_pw1, conv_b_pw1, conv_w_dw, conv_b_dw, conv_ln_g, conv_ln_b,
              conv_w_pw2, conv_b_pw2, kv_norm, w_k_shared, w_v_shared,
              diff_w_q, diff_lambda_q1, diff_lambda_k1, diff_lambda_q2, diff_lambda_k2,
              diff_subln_g, diff_w_o, mem_w_q, mem_w_k, mem_w_v, mem_w_o,
              ffn_w_gate, ffn_w_up, ffn_w_down):
    cos, sin = rope_tables(positions)
    k_sh = None
    v_sh = None
    for i in range(DEPTH):
        if i < N_A_LAYERS:
            a = i
            h = rms_norm(x, norm_mix[i])
            x = x + conformer_conv(h, conv_w_pw1[a], conv_b_pw1[a], conv_w_dw[a], conv_b_dw[a],
                                   conv_ln_g[a], conv_ln_b[a], conv_w_pw2[a], conv_b_pw2[a])
        else:
            b = i - N_A_LAYERS
            if b == 0:
                k_sh, v_sh = shared_kv(x, kv_norm, w_k_shared, w_v_shared, cos, sin)
            lambda_init = 0.8 - 0.6 * math.exp(-0.3 * i)
            h = rms_norm(x, norm_mix[i])
            x = x + diff_attention(h, k_sh, v_sh, cos, sin, diff_w_q[b],
                                   diff_lambda_q1[b], diff_lambda_k1[b],
                                   diff_lambda_q2[b], diff_lambda_k2[b],
                                   diff_subln_g[b], diff_w_o[b], lambda_init)
        x = x + memory_cross_attention(rms_norm(x, norm_mem[i]), mem,
                                       mem_w_q[i], mem_w_k[i], mem_w_v[i], mem_w_o[i])
        x = x + swiglu(rms_norm(x, norm_ffn[i]), ffn_w_gate[i], ffn_w_up[i], ffn_w_down[i])
    return rms_norm(x, norm_final)
```

```python
import functools
import math

import jax
import jax.numpy as jnp
from jax import lax
from jax.experimental import pallas as pl
from jax.experimental.pallas import tpu as pltpu

D_MODEL = 1024
DEPTH = 4
N_A_LAYERS = DEPTH // 2
CONV_WIDTH = 31
DIFF_HEADS = 8
DIFF_HEAD_DIM = 64
DIFF_V_DIM = 2 * DIFF_HEAD_DIM
ROT_DIM = DIFF_HEAD_DIM // 4
ROPE_THETA = 500000.0
MEM_HEADS = 4
MEM_HEAD_DIM = D_MODEL // MEM_HEADS
RMS_EPS = 1e-6
LN_EPS = 1e-5
SUBLN_EPS = 1e-5

LANES = 128
SUBLANES = 8
CONV_HALO = 32
ROW_TILE = 512
ATTN_TILE = 512
FFN_CHUNK = 256
VMEM_LIMIT = 56 * 1024 * 1024

BF16 = jnp.bfloat16
F32 = jnp.float32


def _dot(a, b):
    return jnp.dot(a, b, preferred_element_type=F32)


def _dot_nt(a, b):
    return lax.dot_general(a, b, (((1,), (1,)), ((), ())), preferred_element_type=F32)


def _rms(x, g, eps):
    return x * lax.rsqrt(jnp.mean(x * x, axis=-1, keepdims=True) + eps) * g


def _params(n_grid_dims):
    return pltpu.CompilerParams(
        dimension_semantics=("arbitrary",) * n_grid_dims, vmem_limit_bytes=VMEM_LIMIT)


def _resident(shape):
    zeros = (0,) * len(shape)
    return pl.BlockSpec(shape, lambda *_: zeros, pipeline_mode=pl.Buffered(1))


def _rows(tile, width):
    return pl.BlockSpec((tile, width), lambda i: (i, 0))


def _rope_table_kernel(pos_ref, invf_ref, a_ref, b_ref, c_ref):
    ang = pos_ref[...].astype(F32) * invf_ref[...]
    cos = jnp.cos(ang)
    sin = jnp.sin(ang)
    lane = lax.broadcasted_iota(jnp.int32, ang.shape, 1) % DIFF_HEAD_DIM
    half = ROT_DIM // 2
    a_ref[...] = jnp.where(lane < ROT_DIM, cos, 1.0)
    b_ref[...] = jnp.where(lane < half, -sin, 0.0)
    c_ref[...] = jnp.where((lane >= half) & (lane < ROT_DIM), sin, 0.0)


def _rope_tables(positions):
    t = positions.size
    half = ROT_DIM // 2
    inv_freq = ROPE_THETA ** (-jnp.arange(0, ROT_DIM, 2, dtype=F32) / ROT_DIM)
    lane = jnp.arange(LANES) % DIFF_HEAD_DIM
    invf = jnp.where(lane < ROT_DIM, inv_freq[lane % half], 0.0).reshape(1, LANES).astype(F32)
    tile = 2048
    out = jax.ShapeDtypeStruct((t, LANES), F32)
    return pl.pallas_call(
        _rope_table_kernel,
        grid=(t // tile,),
        in_specs=[pl.BlockSpec((tile, 1), lambda i: (i, 0)), _resident((1, LANES))],
        out_specs=[_rows(tile, LANES)] * 3,
        out_shape=[out] * 3,
        compiler_params=_params(1),
        name="rope_tables",
    )(positions.reshape(t, 1), invf)


def _apply_rope(z, a, b, c):
    half = ROT_DIM // 2
    cols = []
    for j in range(z.shape[1] // LANES):
        zj = z[:, j * LANES:(j + 1) * LANES]
        cols.append(zj * a + pltpu.roll(zj, LANES - half, 1) * b + pltpu.roll(zj, half, 1) * c)
    return jnp.concatenate(cols, axis=1)


def _pw1_kernel(x_ref, g_ref, w_ref, b_ref, u_ref):
    h = _rms(x_ref[...], g_ref[...], RMS_EPS).astype(BF16)
    z = _dot(h, w_ref[...]) + b_ref[...]
    u_ref[...] = z[:, :D_MODEL] * jax.nn.sigmoid(z[:, D_MODEL:])


def _pw1(x, g, w, b):
    t = x.shape[0]
    return pl.pallas_call(
        _pw1_kernel,
        grid=(t // ROW_TILE,),
        in_specs=[_rows(ROW_TILE, D_MODEL), _resident((1, D_MODEL)),
                  _resident((D_MODEL, 2 * D_MODEL)), _resident((1, 2 * D_MODEL))],
        out_specs=_rows(ROW_TILE, D_MODEL),
        out_shape=jax.ShapeDtypeStruct((t, D_MODEL), F32),
        compiler_params=_params(1),
        name="conv_pw1_glu",
    )(x, g, w, b)


CONV_ROW_CHUNK = 32
CONV_LANE_CHUNK = 256


def _conv_kernel(u_ref, halo_ref, x_ref, wdw_ref, bdw_ref, lng_ref, lnb_ref, w2_ref, b2_ref,
                 o_ref, ext_ref, y_ref):
    tile = u_ref.shape[0]
    first = pl.program_id(1) == 0
    ext_ref[0:CONV_HALO, :] = jnp.where(first, 0.0, halo_ref[...])
    ext_ref[CONV_HALO:, :] = u_ref[...]
    lead = CONV_HALO - (CONV_WIDTH - 1)

    def row_chunk(r, carry):
        r0 = pl.multiple_of(r * CONV_ROW_CHUNK, CONV_ROW_CHUNK)
        for c in range(D_MODEL // CONV_LANE_CHUNK):
            cs = slice(c * CONV_LANE_CHUNK, (c + 1) * CONV_LANE_CHUNK)
            win = ext_ref[pl.ds(r0, CONV_ROW_CHUNK + CONV_HALO), cs]
            acc = jnp.zeros((CONV_ROW_CHUNK, CONV_LANE_CHUNK), F32)
            for b in range(SUBLANES):
                offsets = [o for o in range(lead, lead + CONV_WIDTH) if o % SUBLANES == b]
                span = CONV_ROW_CHUNK + CONV_HALO - SUBLANES
                shifted = win if b == 0 else win[b:b + span]
                for o in offsets:
                    a8 = o - b
                    acc = acc + shifted[a8:a8 + CONV_ROW_CHUNK] * wdw_ref[o - lead:o - lead + 1, cs]
            y_ref[pl.ds(r0, CONV_ROW_CHUNK), cs] = acc
        return carry

    lax.fori_loop(0, tile // CONV_ROW_CHUNK, row_chunk, 0)

    y = y_ref[...] + bdw_ref[...]
    mu = jnp.mean(y, axis=-1, keepdims=True)
    yc = y - mu
    yn = yc * lax.rsqrt(jnp.mean(yc * yc, axis=-1, keepdims=True) + LN_EPS)
    yn = yn * lng_ref[...] + lnb_ref[...]
    act = (yn * jax.nn.sigmoid(yn)).astype(BF16)
    o_ref[...] = x_ref[...] + _dot(act, w2_ref[...]) + b2_ref[...]


def _conv(u, x, wdw, bdw, lng, lnb, w2, b2, batch, seq):
    t = u.shape[0]
    tiles = seq // ROW_TILE
    halo_per_tile = ROW_TILE // CONV_HALO
    row_spec = pl.BlockSpec((ROW_TILE, D_MODEL), lambda b, i: (b * tiles + i, 0))
    halo_spec = pl.BlockSpec(
        (CONV_HALO, D_MODEL),
        lambda b, i: (jnp.maximum((b * tiles + i) * halo_per_tile - 1, 0), 0))
    return pl.pallas_call(
        _conv_kernel,
        grid=(batch, tiles),
        in_specs=[row_spec, halo_spec, row_spec,
                  _resident((CONV_HALO, D_MODEL)), _resident((1, D_MODEL)), _resident((1, D_MODEL)),
                  _resident((1, D_MODEL)), _resident((D_MODEL, D_MODEL)), _resident((1, D_MODEL))],
        out_specs=row_spec,
        out_shape=jax.ShapeDtypeStruct((t, D_MODEL), F32),
        scratch_shapes=[pltpu.VMEM((ROW_TILE + CONV_HALO, D_MODEL), F32),
                        pltpu.VMEM((ROW_TILE, D_MODEL), F32)],
        compiler_params=_params(2),
        name="conv_dw_ln_pw2",
    )(u, u, x, wdw, bdw, lng, lnb, w2, b2)


def _memkv_kernel(mem_ref, wk_ref, wv_ref, k_ref, v_ref):
    m = mem_ref[...].astype(BF16)
    k_ref[0] = _dot(m, wk_ref[0]).astype(BF16)
    v_ref[0] = _dot(m, wv_ref[0]).astype(BF16)


def _memkv(mem2d, wk, wv):
    rows = mem2d.shape[0]
    wspec = pl.BlockSpec((1, D_MODEL, D_MODEL), lambda i: (i, 0, 0))
    ospec = pl.BlockSpec((1, rows, D_MODEL), lambda i: (i, 0, 0))
    out = jax.ShapeDtypeStruct((DEPTH, rows, D_MODEL), BF16)
    return pl.pallas_call(
        _memkv_kernel,
        grid=(DEPTH,),
        in_specs=[_resident((rows, D_MODEL)), wspec, wspec],
        out_specs=[ospec, ospec],
        out_shape=[out, out],
        compiler_params=_params(1),
        name="mem_kv_proj",
    )(mem2d, wk, wv)


def _memattn_kernel(x_ref, g_ref, wq_ref, k_ref, v_ref, wo_ref, o_ref):
    x = x_ref[...]
    h = _rms(x, g_ref[...], RMS_EPS).astype(BF16)
    q = (_dot(h, wq_ref[...]) * (MEM_HEAD_DIM ** -0.5)).astype(BF16)
    heads = []
    for hd in range(MEM_HEADS):
        cs = slice(hd * MEM_HEAD_DIM, (hd + 1) * MEM_HEAD_DIM)
        s = _dot_nt(q[:, cs], k_ref[0, 0, :, cs])
        e = jnp.exp(s - jnp.max(s, axis=-1, keepdims=True))
        p = e / jnp.sum(e, axis=-1, keepdims=True)
        heads.append(_dot(p.astype(BF16), v_ref[0, 0, :, cs]).astype(BF16))
    o = jnp.concatenate(heads, axis=-1)
    o_ref[...] = x + _dot(o, wo_ref[...])


def _memattn(x, g, wq, k_all, v_all, wo, layer, batch, seq):
    t = x.shape[0]
    tiles = seq // ROW_TILE
    m = k_all.shape[2]
    row_spec = pl.BlockSpec((ROW_TILE, D_MODEL), lambda b, i: (b * tiles + i, 0))
    kv_spec = pl.BlockSpec((1, 1, m, D_MODEL), lambda b, i: (layer, b, 0, 0))
    return pl.pallas_call(
        _memattn_kernel,
        grid=(batch, tiles),
        in_specs=[row_spec, _resident((1, D_MODEL)), _resident((D_MODEL, D_MODEL)),
                  kv_spec, kv_spec, _resident((D_MODEL, D_MODEL))],
        out_specs=row_spec,
        out_shape=jax.ShapeDtypeStruct((t, D_MODEL), F32),
        compiler_params=_params(2),
        name="mem_cross_attn",
    )(x, g, wq, k_all, v_all, wo)


def _ffn_kernel(x_ref, g_ref, wg_ref, wu_ref, wd_ref, gf_ref, o_ref, act_ref, *, final_norm):
    x = x_ref[...]
    h = _rms(x, g_ref[...], RMS_EPS).astype(BF16)
    hidden = wg_ref.shape[1]
    for c in range(hidden // FFN_CHUNK):
        cs = slice(c * FFN_CHUNK, (c + 1) * FFN_CHUNK)
        gate = _dot(h, wg_ref[:, cs])
        up = _dot(h, wu_ref[:, cs])
        act_ref[:, cs] = (gate * jax.nn.sigmoid(gate) * up).astype(BF16)
    y = x + _dot(act_ref[...], wd_ref[...])
    if final_norm:
        y = _rms(y, gf_ref[...], RMS_EPS)
    o_ref[...] = y


def _ffn(x, g, wg, wu, wd, g_final, final_norm):
    t = x.shape[0]
    hidden = wg.shape[1]
    return pl.pallas_call(
        functools.partial(_ffn_kernel, final_norm=final_norm),
        grid=(t // ROW_TILE,),
        in_specs=[_rows(ROW_TILE, D_MODEL), _resident((1, D_MODEL)),
                  _resident((D_MODEL, hidden)), _resident((D_MODEL, hidden)),
                  _resident((hidden, D_MODEL)), _resident((1, D_MODEL))],
        out_specs=_rows(ROW_TILE, D_MODEL),
        out_shape=jax.ShapeDtypeStruct((t, D_MODEL), F32),
        scratch_shapes=[pltpu.VMEM((ROW_TILE, hidden), BF16)],
        compiler_params=_params(1),
        name="swiglu_ffn",
    )(x, g, wg, wu, wd, g_final)


def _kv_kernel(x_ref, g_ref, wk_ref, wv_ref, a_ref, b_ref, c_ref, k_ref, v_ref):
    h = _rms(x_ref[...], g_ref[...], RMS_EPS).astype(BF16)
    k = _apply_rope(_dot(h, wk_ref[...]), a_ref[...], b_ref[...], c_ref[...])
    k_ref[...] = k.astype(BF16)
    v_ref[...] = _dot(h, wv_ref[...]).astype(BF16)


def _shared_kv(x, g, wk, wv, tabs):
    t = x.shape[0]
    out = jax.ShapeDtypeStruct((t, D_MODEL), BF16)
    return pl.pallas_call(
        _kv_kernel,
        grid=(t // ROW_TILE,),
        in_specs=[_rows(ROW_TILE, D_MODEL), _resident((1, D_MODEL)),
                  _resident((D_MODEL, D_MODEL)), _resident((D_MODEL, D_MODEL))]
                 + [_rows(ROW_TILE, LANES)] * 3,
        out_specs=[_rows(ROW_TILE, D_MODEL)] * 2,
        out_shape=[out, out],
        compiler_params=_params(1),
        name="shared_kv_proj",
    )(x, g, wk, wv, *tabs)


def _q_kernel(x_ref, g_ref, wq_ref, a_ref, b_ref, c_ref, q_ref):
    h = _rms(x_ref[...], g_ref[...], RMS_EPS).astype(BF16)
    q = _apply_rope(_dot(h, wq_ref[...]), a_ref[...], b_ref[...], c_ref[...])
    q_ref[...] = (q * (DIFF_HEAD_DIM ** -0.5)).astype(BF16)


def _q_proj(x, g, wq, tabs):
    t = x.shape[0]
    return pl.pallas_call(
        _q_kernel,
        grid=(t // ROW_TILE,),
        in_specs=[_rows(ROW_TILE, D_MODEL), _resident((1, D_MODEL)), _resident((D_MODEL, D_MODEL))]
                 + [_rows(ROW_TILE, LANES)] * 3,
        out_specs=_rows(ROW_TILE, D_MODEL),
        out_shape=jax.ShapeDtypeStruct((t, D_MODEL), BF16),
        compiler_params=_params(1),
        name="diff_q_proj",
    )(x, g, wq, *tabs)


def _diff_attn_kernel(lam_ref, q_ref, k_ref, v_ref, g_ref, o_ref, m_ref, l_ref, acc_ref,
                      *, lambda_init):
    tq = q_ref.shape[0]
    qi = pl.program_id(2)
    q = q_ref[...]
    lane = lax.broadcasted_iota(jnp.int32, q.shape, 1)
    zero = jnp.zeros_like(q)
    qs = jnp.concatenate([jnp.where(lane < DIFF_HEAD_DIM, q, zero),
                          jnp.where(lane >= DIFF_HEAD_DIM, q, zero)], axis=0)
    m_ref[...] = jnp.full(m_ref.shape, -jnp.inf, F32)
    l_ref[...] = jnp.zeros(l_ref.shape, F32)
    acc_ref[...] = jnp.zeros(acc_ref.shape, F32)

    def step(j, causal):
        start = pl.multiple_of(j * tq, tq)
        s = _dot_nt(qs, k_ref[pl.ds(start, tq), :])
        if causal:
            row = lax.broadcasted_iota(jnp.int32, s.shape, 0) % tq
            col = lax.broadcasted_iota(jnp.int32, s.shape, 1)
            s = jnp.where(col <= row, s, -jnp.inf)
        m_prev = m_ref[...]
        m_new = jnp.maximum(m_prev, jnp.max(s, axis=-1, keepdims=True))
        alpha = jnp.exp(m_prev - m_new)
        p = jnp.exp(s - m_new)
        l_ref[...] = alpha * l_ref[...] + jnp.sum(p, axis=-1, keepdims=True)
        acc_ref[...] = alpha * acc_ref[...] + _dot(p.astype(BF16), v_ref[pl.ds(start, tq), :])
        m_ref[...] = m_new

    def full_block(j, carry):
        step(j, False)
        return carry

    lax.fori_loop(0, qi, full_block, 0)
    step(qi, True)

    lam_rows = lam_ref[...]
    lam = (jnp.exp(jnp.sum(lam_rows[0:1] * lam_rows[1:2], axis=-1, keepdims=True))
           - jnp.exp(jnp.sum(lam_rows[2:3] * lam_rows[3:4], axis=-1, keepdims=True))
           + lambda_init)
    o = acc_ref[...] / l_ref[...]
    o = o[:tq] - lam * o[tq:]
    o = _rms(o, g_ref[...], SUBLN_EPS) * (1.0 - lambda_init)
    o_ref[...] = o.astype(BF16)


def _diff_attn(q, k, v, lam_rows, subln_g, lambda_init, batch, seq):
    t = q.shape[0]
    tq = ATTN_TILE
    nq = seq // tq
    q_spec = pl.BlockSpec((tq, DIFF_V_DIM), lambda b, h, i: (b * nq + i, h))
    kv_spec = pl.BlockSpec((seq, DIFF_V_DIM), lambda b, h, i: (b, h))
    return pl.pallas_call(
        functools.partial(_diff_attn_kernel, lambda_init=lambda_init),
        grid=(batch, DIFF_HEADS, nq),
        in_specs=[_resident((8, LANES)), q_spec, kv_spec, kv_spec, _resident((1, DIFF_V_DIM))],
        out_specs=q_spec,
        out_shape=jax.ShapeDtypeStruct((t, D_MODEL), BF16),
        scratch_shapes=[pltpu.VMEM((2 * tq, 1), F32), pltpu.VMEM((2 * tq, 1), F32),
                        pltpu.VMEM((2 * tq, DIFF_V_DIM), F32)],
        compiler_params=_params(3),
        name="diff_attn",
    )(lam_rows, q, k, v, subln_g)


def _wo_kernel(o_ref, w_ref, x_ref, y_ref):
    y_ref[...] = x_ref[...] + _dot(o_ref[...], w_ref[...])


def _out_proj(o, w, x):
    t = x.shape[0]
    return pl.pallas_call(
        _wo_kernel,
        grid=(t // ROW_TILE,),
        in_specs=[_rows(ROW_TILE, D_MODEL), _resident((D_MODEL, D_MODEL)), _rows(ROW_TILE, D_MODEL)],
        out_specs=_rows(ROW_TILE, D_MODEL),
        out_shape=jax.ShapeDtypeStruct((t, D_MODEL), F32),
        compiler_params=_params(1),
        name="diff_out_proj",
    )(o, w, x)


def kernel(x, mem, positions, norm_mix, norm_mem, norm_ffn, norm_final, conv_w_pw1, conv_b_pw1, conv_w_dw, conv_b_dw, conv_ln_g, conv_ln_b, conv_w_pw2, conv_b_pw2, kv_norm, w_k_shared, w_v_shared, diff_w_q, diff_lambda_q1, diff_lambda_k1, diff_lambda_q2, diff_lambda_k2, diff_subln_g, diff_w_o, mem_w_q, mem_w_k, mem_w_v, mem_w_o, ffn_w_gate, ffn_w_up, ffn_w_down):
    batch, seq, d = x.shape
    t = batch * seq
    row = lambda v: v.reshape(1, -1).astype(F32)
    bf = lambda w: w.astype(BF16)

    xs = x.reshape(t, d)
    tabs = _rope_tables(positions)
    mem_k, mem_v = _memkv(mem.reshape(-1, d), bf(mem_w_k), bf(mem_w_v))
    mem_k = mem_k.reshape(DEPTH, batch, -1, d)
    mem_v = mem_v.reshape(DEPTH, batch, -1, d)

    k_sh = v_sh = None
    for i in range(DEPTH):
        if i < N_A_LAYERS:
            u = _pw1(xs, row(norm_mix[i]), bf(conv_w_pw1[i]), row(conv_b_pw1[i]))
            wdw = jnp.pad(conv_w_dw[i], ((0, CONV_HALO - CONV_WIDTH), (0, 0)))
            xs = _conv(u, xs, wdw, row(conv_b_dw[i]), row(conv_ln_g[i]), row(conv_ln_b[i]),
                       bf(conv_w_pw2[i]), row(conv_b_pw2[i]), batch, seq)
        else:
            b = i - N_A_LAYERS
            if b == 0:
                k_sh, v_sh = _shared_kv(xs, row(kv_norm), bf(w_k_shared), bf(w_v_shared), tabs)
            lambda_init = 0.8 - 0.6 * math.exp(-0.3 * i)
            q = _q_proj(xs, row(norm_mix[i]), bf(diff_w_q[b]), tabs)
            lam_rows = jnp.stack([diff_lambda_q1[b], diff_lambda_k1[b],
                                  diff_lambda_q2[b], diff_lambda_k2[b]]).astype(F32)
            lam_rows = jnp.pad(lam_rows, ((0, 4), (0, LANES - DIFF_HEAD_DIM)))
            o = _diff_attn(q, k_sh, v_sh, lam_rows, row(diff_subln_g[b]), lambda_init, batch, seq)
            xs = _out_proj(o, bf(diff_w_o[b]), xs)
        xs = _memattn(xs, row(norm_mem[i]), bf(mem_w_q[i]), mem_k, mem_v, bf(mem_w_o[i]),
                      i, batch, seq)
        xs = _ffn(xs, row(norm_ffn[i]), bf(ffn_w_gate[i]), bf(ffn_w_up[i]), bf(ffn_w_down[i]),
                  row(norm_final), final_norm=(i == DEPTH - 1))
    return xs.reshape(batch, seq, d)
```

```python
import functools
import math

import jax
import jax.numpy as jnp
from jax import lax
from jax.experimental import pallas as pl
from jax.experimental.pallas import tpu as pltpu

D_MODEL = 1024
DEPTH = 4
N_A_LAYERS = DEPTH // 2
CONV_WIDTH = 31
DIFF_HEADS = 8
DIFF_HEAD_DIM = 64
DIFF_V_DIM = 2 * DIFF_HEAD_DIM
ROT_DIM = DIFF_HEAD_DIM // 4
ROPE_THETA = 500000.0
MEM_HEADS = 4
MEM_HEAD_DIM = D_MODEL // MEM_HEADS
RMS_EPS = 1e-6
LN_EPS = 1e-5
SUBLN_EPS = 1e-5
LOG2_E = math.log2(math.e)

LANES = 128
SUBLANES = 8
CONV_HALO = 32
ROW_TILE = 512
ATTN_TILE = 512
ATTN_ROW_CHUNK = 512
FFN_CHUNK = 256
VMEM_LIMIT = 56 * 1024 * 1024

BF16 = jnp.bfloat16
F32 = jnp.float32


def _dot(a, b):
    return jnp.dot(a, b, preferred_element_type=F32)


def _dot_nt(a, b):
    return lax.dot_general(a, b, (((1,), (1,)), ((), ())), preferred_element_type=F32)


def _rms(x, g, eps):
    return x * lax.rsqrt(jnp.mean(x * x, axis=-1, keepdims=True) + eps) * g


def _params(n_grid_dims):
    return pltpu.CompilerParams(
        dimension_semantics=("arbitrary",) * n_grid_dims, vmem_limit_bytes=VMEM_LIMIT)


def _resident(shape):
    zeros = (0,) * len(shape)
    return pl.BlockSpec(shape, lambda *_: zeros, pipeline_mode=pl.Buffered(1))


def _rows(tile, width):
    return pl.BlockSpec((tile, width), lambda i: (i, 0))


def _rope_table_kernel(pos_ref, invf_ref, a_ref, b_ref, c_ref):
    ang = pos_ref[...].astype(F32) * invf_ref[...]
    cos = jnp.cos(ang)
    sin = jnp.sin(ang)
    lane = lax.broadcasted_iota(jnp.int32, ang.shape, 1) % DIFF_HEAD_DIM
    half = ROT_DIM // 2
    a_ref[...] = jnp.where(lane < ROT_DIM, cos, 1.0)
    b_ref[...] = jnp.where(lane < half, -sin, 0.0)
    c_ref[...] = jnp.where((lane >= half) & (lane < ROT_DIM), sin, 0.0)


def _rope_tables(positions):
    t = positions.size
    half = ROT_DIM // 2
    inv_freq = ROPE_THETA ** (-jnp.arange(0, ROT_DIM, 2, dtype=F32) / ROT_DIM)
    lane = jnp.arange(LANES) % DIFF_HEAD_DIM
    invf = jnp.where(lane < ROT_DIM, inv_freq[lane % half], 0.0).reshape(1, LANES).astype(F32)
    tile = 2048
    out = jax.ShapeDtypeStruct((t, LANES), F32)
    return pl.pallas_call(
        _rope_table_kernel,
        grid=(t // tile,),
        in_specs=[pl.BlockSpec((tile, 1), lambda i: (i, 0)), _resident((1, LANES))],
        out_specs=[_rows(tile, LANES)] * 3,
        out_shape=[out] * 3,
        compiler_params=_params(1),
        name="rope_tables",
    )(positions.reshape(t, 1), invf)


def _apply_rope(z, a, b, c):
    half = ROT_DIM // 2
    cols = []
    for j in range(z.shape[1] // LANES):
        zj = z[:, j * LANES:(j + 1) * LANES]
        cols.append(zj * a + pltpu.roll(zj, LANES - half, 1) * b + pltpu.roll(zj, half, 1) * c)
    return jnp.concatenate(cols, axis=1)


def _pw1_kernel(x_ref, g_ref, w_ref, b_ref, u_ref):
    h = _rms(x_ref[...], g_ref[...], RMS_EPS).astype(BF16)
    z = _dot(h, w_ref[...]) + b_ref[...]
    u_ref[...] = z[:, :D_MODEL] * jax.nn.sigmoid(z[:, D_MODEL:])


def _pw1(x, g, w, b):
    t = x.shape[0]
    return pl.pallas_call(
        _pw1_kernel,
        grid=(t // ROW_TILE,),
        in_specs=[_rows(ROW_TILE, D_MODEL), _resident((1, D_MODEL)),
                  _resident((D_MODEL, 2 * D_MODEL)), _resident((1, 2 * D_MODEL))],
        out_specs=_rows(ROW_TILE, D_MODEL),
        out_shape=jax.ShapeDtypeStruct((t, D_MODEL), F32),
        compiler_params=_params(1),
        name="conv_pw1_glu",
    )(x, g, w, b)


CONV_ROW_CHUNK = 64
CONV_LANE_CHUNK = 128


def _conv_kernel(u_ref, halo_ref, x_ref, wdw_ref, bdw_ref, lng_ref, lnb_ref, w2_ref, b2_ref,
                 o_ref, ext_ref, y_ref):
    tile = u_ref.shape[0]
    first = pl.program_id(1) == 0
    ext_ref[0:CONV_HALO, :] = jnp.where(first, 0.0, halo_ref[...])
    ext_ref[CONV_HALO:, :] = u_ref[...]
    lead = CONV_HALO - (CONV_WIDTH - 1)

    def row_chunk(r, carry):
        r0 = pl.multiple_of(r * CONV_ROW_CHUNK, CONV_ROW_CHUNK)
        for c in range(D_MODEL // CONV_LANE_CHUNK):
            cs = slice(c * CONV_LANE_CHUNK, (c + 1) * CONV_LANE_CHUNK)
            win = ext_ref[pl.ds(r0, CONV_ROW_CHUNK + CONV_HALO), cs]
            acc = jnp.zeros((CONV_ROW_CHUNK, CONV_LANE_CHUNK), F32)
            for b in range(SUBLANES):
                offsets = [o for o in range(lead, lead + CONV_WIDTH) if o % SUBLANES == b]
                span = CONV_ROW_CHUNK + CONV_HALO - SUBLANES
                shifted = win if b == 0 else pltpu.roll(win, win.shape[0] - b, 0)[0:span]
                for o in offsets:
                    a8 = o - b
                    tap = (o - lead) * SUBLANES
                    w8 = wdw_ref[tap:tap + SUBLANES, cs]
                    w_rows = jnp.concatenate([w8] * (CONV_ROW_CHUNK // SUBLANES), axis=0)
                    acc = acc + shifted[a8:a8 + CONV_ROW_CHUNK] * w_rows
            y_ref[pl.ds(r0, CONV_ROW_CHUNK), cs] = acc
        return carry

    lax.fori_loop(0, tile // CONV_ROW_CHUNK, row_chunk, 0)

    y = y_ref[...] + bdw_ref[...]
    mu = jnp.mean(y, axis=-1, keepdims=True)
    yc = y - mu
    yn = yc * lax.rsqrt(jnp.mean(yc * yc, axis=-1, keepdims=True) + LN_EPS)
    yn = yn * lng_ref[...] + lnb_ref[...]
    act = (yn * jax.nn.sigmoid(yn)).astype(BF16)
    o_ref[...] = x_ref[...] + _dot(act, w2_ref[...]) + b2_ref[...]


def _conv(u, x, wdw, bdw, lng, lnb, w2, b2, batch, seq):
    t = u.shape[0]
    tiles = seq // ROW_TILE
    halo_per_tile = ROW_TILE // CONV_HALO
    row_spec = pl.BlockSpec((ROW_TILE, D_MODEL), lambda b, i: (b * tiles + i, 0))
    halo_spec = pl.BlockSpec(
        (CONV_HALO, D_MODEL),
        lambda b, i: (jnp.maximum((b * tiles + i) * halo_per_tile - 1, 0), 0))
    return pl.pallas_call(
        _conv_kernel,
        grid=(batch, tiles),
        in_specs=[row_spec, halo_spec, row_spec,
                  _resident((CONV_WIDTH * SUBLANES, D_MODEL)), _resident((1, D_MODEL)),
                  _resident((1, D_MODEL)),
                  _resident((1, D_MODEL)), _resident((D_MODEL, D_MODEL)), _resident((1, D_MODEL))],
        out_specs=row_spec,
        out_shape=jax.ShapeDtypeStruct((t, D_MODEL), F32),
        scratch_shapes=[pltpu.VMEM((ROW_TILE + CONV_HALO, D_MODEL), F32),
                        pltpu.VMEM((ROW_TILE, D_MODEL), F32)],
        compiler_params=_params(2),
        name="conv_dw_ln_pw2",
    )(u, u, x, wdw, bdw, lng, lnb, w2, b2)


def _memkv_kernel(mem_ref, wk_ref, wv_ref, k_ref, v_ref):
    m = mem_ref[...].astype(BF16)
    k_ref[0] = _dot(m, wk_ref[0]).astype(BF16)
    v_ref[0] = _dot(m, wv_ref[0]).astype(BF16)


def _memkv(mem2d, wk, wv):
    rows = mem2d.shape[0]
    wspec = pl.BlockSpec((1, D_MODEL, D_MODEL), lambda i: (i, 0, 0))
    ospec = pl.BlockSpec((1, rows, D_MODEL), lambda i: (i, 0, 0))
    out = jax.ShapeDtypeStruct((DEPTH, rows, D_MODEL), BF16)
    return pl.pallas_call(
        _memkv_kernel,
        grid=(DEPTH,),
        in_specs=[_resident((rows, D_MODEL)), wspec, wspec],
        out_specs=[ospec, ospec],
        out_shape=[out, out],
        compiler_params=_params(1),
        name="mem_kv_proj",
    )(mem2d, wk, wv)


def _memattn_kernel(x_ref, g_ref, wq_ref, k_ref, v_ref, wo_ref, o_ref):
    x = x_ref[...]
    h = _rms(x, g_ref[...], RMS_EPS).astype(BF16)
    q = (_dot(h, wq_ref[...]) * (MEM_HEAD_DIM ** -0.5)).astype(BF16)
    heads = []
    for hd in range(MEM_HEADS):
        cs = slice(hd * MEM_HEAD_DIM, (hd + 1) * MEM_HEAD_DIM)
        s = _dot_nt(q[:, cs], k_ref[0, 0, :, cs])
        e = jnp.exp(s - jnp.max(s, axis=-1, keepdims=True))
        p = e / jnp.sum(e, axis=-1, keepdims=True)
        heads.append(_dot(p.astype(BF16), v_ref[0, 0, :, cs]).astype(BF16))
    o = jnp.concatenate(heads, axis=-1)
    o_ref[...] = x + _dot(o, wo_ref[...])


def _memattn(x, g, wq, k_all, v_all, wo, layer, batch, seq):
    t = x.shape[0]
    tiles = seq // ROW_TILE
    m = k_all.shape[2]
    row_spec = pl.BlockSpec((ROW_TILE, D_MODEL), lambda b, i: (b * tiles + i, 0))
    kv_spec = pl.BlockSpec((1, 1, m, D_MODEL), lambda b, i: (layer, b, 0, 0))
    return pl.pallas_call(
        _memattn_kernel,
        grid=(batch, tiles),
        in_specs=[row_spec, _resident((1, D_MODEL)), _resident((D_MODEL, D_MODEL)),
                  kv_spec, kv_spec, _resident((D_MODEL, D_MODEL))],
        out_specs=row_spec,
        out_shape=jax.ShapeDtypeStruct((t, D_MODEL), F32),
        compiler_params=_params(2),
        name="mem_cross_attn",
    )(x, g, wq, k_all, v_all, wo)


def _ffn_kernel(x_ref, g_ref, wg_ref, wu_ref, wd_ref, gf_ref, o_ref, act_ref, *, final_norm):
    x = x_ref[...]
    h = _rms(x, g_ref[...], RMS_EPS).astype(BF16)
    hidden = wg_ref.shape[1]
    for c in range(hidden // FFN_CHUNK):
        cs = slice(c * FFN_CHUNK, (c + 1) * FFN_CHUNK)
        gate = _dot(h, wg_ref[:, cs])
        up = _dot(h, wu_ref[:, cs])
        act_ref[:, cs] = (gate * jax.nn.sigmoid(gate) * up).astype(BF16)
    y = x + _dot(act_ref[...], wd_ref[...])
    if final_norm:
        y = _rms(y, gf_ref[...], RMS_EPS)
    o_ref[...] = y


def _ffn(x, g, wg, wu, wd, g_final, final_norm):
    t = x.shape[0]
    hidden = wg.shape[1]
    return pl.pallas_call(
        functools.partial(_ffn_kernel, final_norm=final_norm),
        grid=(t // ROW_TILE,),
        in_specs=[_rows(ROW_TILE, D_MODEL), _resident((1, D_MODEL)),
                  _resident((D_MODEL, hidden)), _resident((D_MODEL, hidden)),
                  _resident((hidden, D_MODEL)), _resident((1, D_MODEL))],
        out_specs=_rows(ROW_TILE, D_MODEL),
        out_shape=jax.ShapeDtypeStruct((t, D_MODEL), F32),
        scratch_shapes=[pltpu.VMEM((ROW_TILE, hidden), BF16)],
        compiler_params=_params(1),
        name="swiglu_ffn",
    )(x, g, wg, wu, wd, g_final)


def _kv_kernel(x_ref, g_ref, wk_ref, wv_ref, a_ref, b_ref, c_ref, k_ref, v_ref):
    h = _rms(x_ref[...], g_ref[...], RMS_EPS).astype(BF16)
    k = _apply_rope(_dot(h, wk_ref[...]), a_ref[...], b_ref[...], c_ref[...])
    k_ref[...] = k.astype(BF16)
    v_ref[...] = _dot(h, wv_ref[...]).astype(BF16)


def _shared_kv(x, g, wk, wv, tabs):
    t = x.shape[0]
    out = jax.ShapeDtypeStruct((t, D_MODEL), BF16)
    return pl.pallas_call(
        _kv_kernel,
        grid=(t // ROW_TILE,),
        in_specs=[_rows(ROW_TILE, D_MODEL), _resident((1, D_MODEL)),
                  _resident((D_MODEL, D_MODEL)), _resident((D_MODEL, D_MODEL))]
                 + [_rows(ROW_TILE, LANES)] * 3,
        out_specs=[_rows(ROW_TILE, D_MODEL)] * 2,
        out_shape=[out, out],
        compiler_params=_params(1),
        name="shared_kv_proj",
    )(x, g, wk, wv, *tabs)


def _q_kernel(x_ref, g_ref, wq_ref, a_ref, b_ref, c_ref, q_ref):
    h = _rms(x_ref[...], g_ref[...], RMS_EPS).astype(BF16)
    q = _apply_rope(_dot(h, wq_ref[...]), a_ref[...], b_ref[...], c_ref[...])
    q_ref[...] = (q * (DIFF_HEAD_DIM ** -0.5 * LOG2_E)).astype(BF16)


def _q_proj(x, g, wq, tabs):
    t = x.shape[0]
    return pl.pallas_call(
        _q_kernel,
        grid=(t // ROW_TILE,),
        in_specs=[_rows(ROW_TILE, D_MODEL), _resident((1, D_MODEL)), _resident((D_MODEL, D_MODEL))]
                 + [_rows(ROW_TILE, LANES)] * 3,
        out_specs=_rows(ROW_TILE, D_MODEL),
        out_shape=jax.ShapeDtypeStruct((t, D_MODEL), BF16),
        compiler_params=_params(1),
        name="diff_q_proj",
    )(x, g, wq, *tabs)


def _diff_attn_kernel(lam_ref, q_ref, k_ref, v_ref, g_ref, o_ref,
                      qs_ref, m_ref, l_ref, acc_ref, *, lambda_init):
    tq = q_ref.shape[0]
    qi = pl.program_id(2)
    q = q_ref[...]
    lane = lax.broadcasted_iota(jnp.int32, q.shape, 1)
    zero = jnp.zeros_like(q)
    qs_ref[0:tq, :] = jnp.where(lane < DIFF_HEAD_DIM, q, zero)
    qs_ref[tq:, :] = jnp.where(lane >= DIFF_HEAD_DIM, q, zero)
    m_ref[...] = jnp.full(m_ref.shape, -jnp.inf, F32)
    l_ref[...] = jnp.zeros(l_ref.shape, F32)
    acc_ref[...] = jnp.zeros(acc_ref.shape, F32)

    n_chunks = 2 * tq // ATTN_ROW_CHUNK

    def chunk_rows(c):
        return slice(c * ATTN_ROW_CHUNK, (c + 1) * ATTN_ROW_CHUNK)

    def step(j, causal):
        start = pl.multiple_of(j * tq, tq)
        k = k_ref[pl.ds(start, tq), :]
        v = v_ref[pl.ds(start, tq), :]

        def scores(c):
            return _dot_nt(qs_ref[chunk_rows(c), :], k)

        def softmax(c, s):
            rows = chunk_rows(c)
            if causal:
                row = lax.broadcasted_iota(jnp.int32, s.shape, 0) + (c * ATTN_ROW_CHUNK) % tq
                col = lax.broadcasted_iota(jnp.int32, s.shape, 1)
                s = jnp.where(col <= row, s, -jnp.inf)
            m_prev = m_ref[rows, :]
            m_new = jnp.maximum(m_prev, jnp.max(s, axis=-1, keepdims=True))
            alpha = jnp.exp2(m_prev - m_new)
            p = jnp.exp2(s - pltpu.repeat(m_new, tq // LANES, axis=1))
            p_lanes = p[:, 0:LANES]
            for t in range(1, tq // LANES):
                p_lanes = p_lanes + p[:, t * LANES:(t + 1) * LANES]
            l_ref[rows, :] = alpha * l_ref[rows, :] + p_lanes
            m_ref[rows, :] = m_new
            return alpha, p.astype(BF16)

        def values(c, alpha, p):
            rows = chunk_rows(c)
            acc_ref[rows, :] = alpha * acc_ref[rows, :] + _dot(p, v)

        s_vals = {0: scores(0)}
        if n_chunks > 1:
            s_vals[1] = scores(1)
        pending = None
        for c in range(n_chunks):
            alpha_p = softmax(c, s_vals.pop(c))
            if c + 2 < n_chunks:
                s_vals[c + 2] = scores(c + 2)
            if pending is not None:
                values(c - 1, *pending)
            pending = alpha_p
        values(n_chunks - 1, *pending)

    def full_block(j, carry):
        step(j, False)
        return carry

    lax.fori_loop(0, qi, full_block, 0)
    step(qi, True)

    lam_rows = lam_ref[...]
    lam = (jnp.exp(jnp.sum(lam_rows[0:1] * lam_rows[1:2], axis=-1, keepdims=True))
           - jnp.exp(jnp.sum(lam_rows[2:3] * lam_rows[3:4], axis=-1, keepdims=True))
           + lambda_init)
    o = acc_ref[...] / jnp.sum(l_ref[...], axis=-1, keepdims=True)
    o = o[:tq] - lam * o[tq:]
    o = _rms(o, g_ref[...], SUBLN_EPS) * (1.0 - lambda_init)
    o_ref[...] = o.astype(BF16)


def _diff_attn(q, k, v, lam_rows, subln_g, lambda_init, batch, seq):
    t = q.shape[0]
    tq = ATTN_TILE
    nq = seq // tq
    q_spec = pl.BlockSpec((tq, DIFF_V_DIM), lambda b, h, i: (b * nq + i, h))
    kv_spec = pl.BlockSpec((seq, DIFF_V_DIM), lambda b, h, i: (b, h))
    return pl.pallas_call(
        functools.partial(_diff_attn_kernel, lambda_init=lambda_init),
        grid=(batch, DIFF_HEADS, nq),
        in_specs=[_resident((8, LANES)), q_spec, kv_spec, kv_spec, _resident((1, DIFF_V_DIM))],
        out_specs=q_spec,
        out_shape=jax.ShapeDtypeStruct((t, D_MODEL), BF16),
        scratch_shapes=[pltpu.VMEM((2 * tq, DIFF_V_DIM), BF16),
                        pltpu.VMEM((2 * tq, LANES), F32), pltpu.VMEM((2 * tq, LANES), F32),
                        pltpu.VMEM((2 * tq, DIFF_V_DIM), F32)],
        compiler_params=_params(3),
        name="diff_attn",
    )(lam_rows, q, k, v, subln_g)


def _wo_kernel(o_ref, w_ref, x_ref, y_ref):
    y_ref[...] = x_ref[...] + _dot(o_ref[...], w_ref[...])


def _out_proj(o, w, x):
    t = x.shape[0]
    return pl.pallas_call(
        _wo_kernel,
        grid=(t // ROW_TILE,),
        in_specs=[_rows(ROW_TILE, D_MODEL), _resident((D_MODEL, D_MODEL)), _rows(ROW_TILE, D_MODEL)],
        out_specs=_rows(ROW_TILE, D_MODEL),
        out_shape=jax.ShapeDtypeStruct((t, D_MODEL), F32),
        compiler_params=_params(1),
        name="diff_out_proj",
    )(o, w, x)


def kernel(x, mem, positions, norm_mix, norm_mem, norm_ffn, norm_final, conv_w_pw1, conv_b_pw1, conv_w_dw, conv_b_dw, conv_ln_g, conv_ln_b, conv_w_pw2, conv_b_pw2, kv_norm, w_k_shared, w_v_shared, diff_w_q, diff_lambda_q1, diff_lambda_k1, diff_lambda_q2, diff_lambda_k2, diff_subln_g, diff_w_o, mem_w_q, mem_w_k, mem_w_v, mem_w_o, ffn_w_gate, ffn_w_up, ffn_w_down):
    batch, seq, d = x.shape
    t = batch * seq
    row = lambda v: v.reshape(1, -1).astype(F32)
    bf = lambda w: w.astype(BF16)

    xs = x.reshape(t, d)
    tabs = _rope_tables(positions)
    mem_k, mem_v = _memkv(mem.reshape(-1, d), bf(mem_w_k), bf(mem_w_v))
    mem_k = mem_k.reshape(DEPTH, batch, -1, d)
    mem_v = mem_v.reshape(DEPTH, batch, -1, d)

    k_sh = v_sh = None
    for i in range(DEPTH):
        if i < N_A_LAYERS:
            u = _pw1(xs, row(norm_mix[i]), bf(conv_w_pw1[i]), row(conv_b_pw1[i]))
            wdw = jnp.repeat(conv_w_dw[i].astype(F32), SUBLANES, axis=0)
            xs = _conv(u, xs, wdw, row(conv_b_dw[i]), row(conv_ln_g[i]), row(conv_ln_b[i]),
                       bf(conv_w_pw2[i]), row(conv_b_pw2[i]), batch, seq)
        else:
            b = i - N_A_LAYERS
            if b == 0:
                k_sh, v_sh = _shared_kv(xs, row(kv_norm), bf(w_k_shared), bf(w_v_shared), tabs)
            lambda_init = 0.8 - 0.6 * math.exp(-0.3 * i)
            q = _q_proj(xs, row(norm_mix[i]), bf(diff_w_q[b]), tabs)
            lam_rows = jnp.stack([diff_lambda_q1[b], diff_lambda_k1[b],
                                  diff_lambda_q2[b], diff_lambda_k2[b]]).astype(F32)
            lam_rows = jnp.pad(lam_rows, ((0, 4), (0, LANES - DIFF_HEAD_DIM)))
            o = _diff_attn(q, k_sh, v_sh, lam_rows, row(diff_subln_g[b]), lambda_init, batch, seq)
            xs = _out_proj(o, bf(diff_w_o[b]), xs)
        xs = _memattn(xs, row(norm_mem[i]), bf(mem_w_q[i]), mem_k, mem_v, bf(mem_w_o[i]),
                      i, batch, seq)
        xs = _ffn(xs, row(norm_ffn[i]), bf(ffn_w_gate[i]), bf(ffn_w_up[i]), bf(ffn_w_down[i]),
                  row(norm_final), final_norm=(i == DEPTH - 1))
    return xs.reshape(batch, seq, d)
```

```python
import functools
import math

import jax
import jax.numpy as jnp
from jax import lax
from jax.experimental import pallas as pl
from jax.experimental.pallas import tpu as pltpu

D_MODEL = 1024
DEPTH = 4
N_A_LAYERS = DEPTH // 2
CONV_WIDTH = 31
DIFF_HEADS = 8
DIFF_HEAD_DIM = 64
DIFF_V_DIM = 2 * DIFF_HEAD_DIM
ROT_DIM = DIFF_HEAD_DIM // 4
ROPE_THETA = 500000.0
MEM_HEADS = 4
MEM_HEAD_DIM = D_MODEL // MEM_HEADS
RMS_EPS = 1e-6
LN_EPS = 1e-5
SUBLN_EPS = 1e-5
LOG2_E = math.log2(math.e)

LANES = 128
SUBLANES = 8
CONV_HALO = 32
ROW_TILE = 1024
SUB_TILE = 128
ATTN_TILE = 512
ATTN_LOOKAHEAD = 2
ATTN_ROW_CHUNK = 512
FFN_CHUNK = 256
VMEM_LIMIT = 56 * 1024 * 1024

BF16 = jnp.bfloat16
F32 = jnp.float32


def _dot(a, b):
    return jnp.dot(a, b, preferred_element_type=F32)


def _dot_nt(a, b):
    return lax.dot_general(a, b, (((1,), (1,)), ((), ())), preferred_element_type=F32)


def _rms(x, g, eps):
    return x * lax.rsqrt(jnp.mean(x * x, axis=-1, keepdims=True) + eps) * g


def _staged_rows(n_rows, stages):
    n = n_rows // SUB_TILE
    depth = len(stages)
    live = {}
    for step in range(n + depth - 1):
        for s, stage in enumerate(stages):
            c = step - s
            if 0 <= c < n:
                rows = slice(c * SUB_TILE, (c + 1) * SUB_TILE)
                live[c] = stage(rows) if s == 0 else stage(rows, live[c])


def _params(n_grid_dims):
    return pltpu.CompilerParams(
        dimension_semantics=("arbitrary",) * n_grid_dims, vmem_limit_bytes=VMEM_LIMIT)


def _resident(shape):
    zeros = (0,) * len(shape)
    return pl.BlockSpec(shape, lambda *_: zeros, pipeline_mode=pl.Buffered(1))


def _rows(tile, width):
    return pl.BlockSpec((tile, width), lambda i: (i, 0))


def _rope_table_kernel(pos_ref, invf_ref, a_ref, b_ref, c_ref):
    ang = pos_ref[...].astype(F32) * invf_ref[...]
    cos = jnp.cos(ang)
    sin = jnp.sin(ang)
    lane = lax.broadcasted_iota(jnp.int32, ang.shape, 1) % DIFF_HEAD_DIM
    half = ROT_DIM // 2
    a_ref[...] = jnp.where(lane < ROT_DIM, cos, 1.0)
    b_ref[...] = jnp.where(lane < half, -sin, 0.0)
    c_ref[...] = jnp.where((lane >= half) & (lane < ROT_DIM), sin, 0.0)


def _rope_tables(positions):
    t = positions.size
    half = ROT_DIM // 2
    inv_freq = ROPE_THETA ** (-jnp.arange(0, ROT_DIM, 2, dtype=F32) / ROT_DIM)
    lane = jnp.arange(LANES) % DIFF_HEAD_DIM
    invf = jnp.where(lane < ROT_DIM, inv_freq[lane % half], 0.0).reshape(1, LANES).astype(F32)
    tile = 2048
    out = jax.ShapeDtypeStruct((t, LANES), F32)
    return pl.pallas_call(
        _rope_table_kernel,
        grid=(t // tile,),
        in_specs=[pl.BlockSpec((tile, 1), lambda i: (i, 0)), _resident((1, LANES))],
        out_specs=[_rows(tile, LANES)] * 3,
        out_shape=[out] * 3,
        compiler_params=_params(1),
        name="rope_tables",
    )(positions.reshape(t, 1), invf)


def _apply_rope(z, a, b, c):
    half = ROT_DIM // 2
    cols = []
    for j in range(z.shape[1] // LANES):
        zj = z[:, j * LANES:(j + 1) * LANES]
        cols.append(zj * a + pltpu.roll(zj, LANES - half, 1) * b + pltpu.roll(zj, half, 1) * c)
    return jnp.concatenate(cols, axis=1)


def _pw1_kernel(x_ref, g_ref, w_ref, b_ref, u_ref):
    h = _rms(x_ref[...], g_ref[...], RMS_EPS).astype(BF16)
    z = _dot(h, w_ref[...]) + b_ref[...]
    u_ref[...] = z[:, :D_MODEL] * jax.nn.sigmoid(z[:, D_MODEL:])


def _pw1(x, g, w, b):
    t = x.shape[0]
    return pl.pallas_call(
        _pw1_kernel,
        grid=(t // ROW_TILE,),
        in_specs=[_rows(ROW_TILE, D_MODEL), _resident((1, D_MODEL)),
                  _resident((D_MODEL, 2 * D_MODEL)), _resident((1, 2 * D_MODEL))],
        out_specs=_rows(ROW_TILE, D_MODEL),
        out_shape=jax.ShapeDtypeStruct((t, D_MODEL), F32),
        compiler_params=_params(1),
        name="conv_pw1_glu",
    )(x, g, w, b)


CONV_ROW_CHUNK = 64
CONV_LANE_CHUNK = 128


def _conv_kernel(u_ref, halo_ref, x_ref, wdw_ref, bdw_ref, lng_ref, lnb_ref, w2_ref, b2_ref,
                 o_ref, ext_ref, y_ref):
    tile = u_ref.shape[0]
    first = pl.program_id(1) == 0
    ext_ref[0:CONV_HALO, :] = jnp.where(first, 0.0, halo_ref[...])
    ext_ref[CONV_HALO:, :] = u_ref[...]
    lead = CONV_HALO - (CONV_WIDTH - 1)

    def row_chunk(r, carry):
        r0 = pl.multiple_of(r * CONV_ROW_CHUNK, CONV_ROW_CHUNK)
        for c in range(D_MODEL // CONV_LANE_CHUNK):
            cs = slice(c * CONV_LANE_CHUNK, (c + 1) * CONV_LANE_CHUNK)
            win = ext_ref[pl.ds(r0, CONV_ROW_CHUNK + CONV_HALO), cs]
            acc = jnp.zeros((CONV_ROW_CHUNK, CONV_LANE_CHUNK), F32)
            for b in range(SUBLANES):
                offsets = [o for o in range(lead, lead + CONV_WIDTH) if o % SUBLANES == b]
                span = CONV_ROW_CHUNK + CONV_HALO - SUBLANES
                shifted = win if b == 0 else pltpu.roll(win, win.shape[0] - b, 0)[0:span]
                for o in offsets:
                    a8 = o - b
                    tap = (o - lead) * SUBLANES
                    w8 = wdw_ref[tap:tap + SUBLANES, cs]
                    w_rows = jnp.concatenate([w8] * (CONV_ROW_CHUNK // SUBLANES), axis=0)
                    acc = acc + shifted[a8:a8 + CONV_ROW_CHUNK] * w_rows
            y_ref[pl.ds(r0, CONV_ROW_CHUNK), cs] = acc
        return carry

    lax.fori_loop(0, tile // CONV_ROW_CHUNK, row_chunk, 0)

    y = y_ref[...] + bdw_ref[...]
    mu = jnp.mean(y, axis=-1, keepdims=True)
    yc = y - mu
    yn = yc * lax.rsqrt(jnp.mean(yc * yc, axis=-1, keepdims=True) + LN_EPS)
    yn = yn * lng_ref[...] + lnb_ref[...]
    act = (yn * jax.nn.sigmoid(yn)).astype(BF16)
    o_ref[...] = x_ref[...] + _dot(act, w2_ref[...]) + b2_ref[...]


def _conv(u, x, wdw, bdw, lng, lnb, w2, b2, batch, seq):
    t = u.shape[0]
    tiles = seq // ROW_TILE
    halo_per_tile = ROW_TILE // CONV_HALO
    row_spec = pl.BlockSpec((ROW_TILE, D_MODEL), lambda b, i: (b * tiles + i, 0))
    halo_spec = pl.BlockSpec(
        (CONV_HALO, D_MODEL),
        lambda b, i: (jnp.maximum((b * tiles + i) * halo_per_tile - 1, 0), 0))
    return pl.pallas_call(
        _conv_kernel,
        grid=(batch, tiles),
        in_specs=[row_spec, halo_spec, row_spec,
                  _resident((CONV_WIDTH * SUBLANES, D_MODEL)), _resident((1, D_MODEL)),
                  _resident((1, D_MODEL)),
                  _resident((1, D_MODEL)), _resident((D_MODEL, D_MODEL)), _resident((1, D_MODEL))],
        out_specs=row_spec,
        out_shape=jax.ShapeDtypeStruct((t, D_MODEL), F32),
        scratch_shapes=[pltpu.VMEM((ROW_TILE + CONV_HALO, D_MODEL), F32),
                        pltpu.VMEM((ROW_TILE, D_MODEL), F32)],
        compiler_params=_params(2),
        name="conv_dw_ln_pw2",
    )(u, u, x, wdw, bdw, lng, lnb, w2, b2)


def _memkv_kernel(mem_ref, wk_ref, wv_ref, k_ref, v_ref):
    m = mem_ref[...].astype(BF16)
    k_ref[0] = _dot(m, wk_ref[0]).astype(BF16)
    v_ref[0] = _dot(m, wv_ref[0]).astype(BF16)


def _memkv(mem2d, wk, wv):
    rows = mem2d.shape[0]
    wspec = pl.BlockSpec((1, D_MODEL, D_MODEL), lambda i: (i, 0, 0))
    ospec = pl.BlockSpec((1, rows, D_MODEL), lambda i: (i, 0, 0))
    out = jax.ShapeDtypeStruct((DEPTH, rows, D_MODEL), BF16)
    return pl.pallas_call(
        _memkv_kernel,
        grid=(DEPTH,),
        in_specs=[_resident((rows, D_MODEL)), wspec, wspec],
        out_specs=[ospec, ospec],
        out_shape=[out, out],
        compiler_params=_params(1),
        name="mem_kv_proj",
    )(mem2d, wk, wv)


def _memattn_kernel(x_ref, g_ref, wq_ref, k_ref, v_ref, wo_ref, o_ref):
    x = x_ref[...]
    h = _rms(x, g_ref[...], RMS_EPS).astype(BF16)
    q = (_dot(h, wq_ref[...]) * (MEM_HEAD_DIM ** -0.5)).astype(BF16)
    heads = []
    for hd in range(MEM_HEADS):
        cs = slice(hd * MEM_HEAD_DIM, (hd + 1) * MEM_HEAD_DIM)
        s = _dot_nt(q[:, cs], k_ref[0, 0, :, cs])
        e = jnp.exp(s - jnp.max(s, axis=-1, keepdims=True))
        p = e / jnp.sum(e, axis=-1, keepdims=True)
        heads.append(_dot(p.astype(BF16), v_ref[0, 0, :, cs]).astype(BF16))
    o = jnp.concatenate(heads, axis=-1)
    o_ref[...] = x + _dot(o, wo_ref[...])


def _memattn(x, g, wq, k_all, v_all, wo, layer, batch, seq):
    t = x.shape[0]
    tiles = seq // ROW_TILE
    m = k_all.shape[2]
    row_spec = pl.BlockSpec((ROW_TILE, D_MODEL), lambda b, i: (b * tiles + i, 0))
    kv_spec = pl.BlockSpec((1, 1, m, D_MODEL), lambda b, i: (layer, b, 0, 0))
    return pl.pallas_call(
        _memattn_kernel,
        grid=(batch, tiles),
        in_specs=[row_spec, _resident((1, D_MODEL)), _resident((D_MODEL, D_MODEL)),
                  kv_spec, kv_spec, _resident((D_MODEL, D_MODEL))],
        out_specs=row_spec,
        out_shape=jax.ShapeDtypeStruct((t, D_MODEL), F32),
        compiler_params=_params(2),
        name="mem_cross_attn",
    )(x, g, wq, k_all, v_all, wo)


def _ffn_kernel(x_ref, g_ref, wg_ref, wu_ref, wd_ref, gf_ref, o_ref, act_ref, *, final_norm):
    x = x_ref[...]
    h = _rms(x, g_ref[...], RMS_EPS).astype(BF16)
    hidden = wg_ref.shape[1]
    for c in range(hidden // FFN_CHUNK):
        cs = slice(c * FFN_CHUNK, (c + 1) * FFN_CHUNK)
        gate = _dot(h, wg_ref[:, cs])
        up = _dot(h, wu_ref[:, cs])
        act_ref[:, cs] = (gate * jax.nn.sigmoid(gate) * up).astype(BF16)
    y = x + _dot(act_ref[...], wd_ref[...])
    if final_norm:
        y = _rms(y, gf_ref[...], RMS_EPS)
    o_ref[...] = y


def _ffn(x, g, wg, wu, wd, g_final, final_norm):
    t = x.shape[0]
    hidden = wg.shape[1]
    return pl.pallas_call(
        functools.partial(_ffn_kernel, final_norm=final_norm),
        grid=(t // ROW_TILE,),
        in_specs=[_rows(ROW_TILE, D_MODEL), _resident((1, D_MODEL)),
                  _resident((D_MODEL, hidden)), _resident((D_MODEL, hidden)),
                  _resident((hidden, D_MODEL)), _resident((1, D_MODEL))],
        out_specs=_rows(ROW_TILE, D_MODEL),
        out_shape=jax.ShapeDtypeStruct((t, D_MODEL), F32),
        scratch_shapes=[pltpu.VMEM((ROW_TILE, hidden), BF16)],
        compiler_params=_params(1),
        name="swiglu_ffn",
    )(x, g, wg, wu, wd, g_final)


def _kv_kernel(x_ref, g_ref, wk_ref, wv_ref, a_ref, b_ref, c_ref, k_ref, v_ref):
    h = _rms(x_ref[...], g_ref[...], RMS_EPS).astype(BF16)
    k = _apply_rope(_dot(h, wk_ref[...]), a_ref[...], b_ref[...], c_ref[...])
    k_ref[...] = k.astype(BF16)
    v_ref[...] = _dot(h, wv_ref[...]).astype(BF16)


def _shared_kv(x, g, wk, wv, tabs):
    t = x.shape[0]
    out = jax.ShapeDtypeStruct((t, D_MODEL), BF16)
    return pl.pallas_call(
        _kv_kernel,
        grid=(t // ROW_TILE,),
        in_specs=[_rows(ROW_TILE, D_MODEL), _resident((1, D_MODEL)),
                  _resident((D_MODEL, D_MODEL)), _resident((D_MODEL, D_MODEL))]
                 + [_rows(ROW_TILE, LANES)] * 3,
        out_specs=[_rows(ROW_TILE, D_MODEL)] * 2,
        out_shape=[out, out],
        compiler_params=_params(1),
        name="shared_kv_proj",
    )(x, g, wk, wv, *tabs)


def _q_kernel(x_ref, g_ref, wq_ref, a_ref, b_ref, c_ref, q_ref):
    def norm(rows):
        return _rms(x_ref[rows, :], g_ref[...], RMS_EPS).astype(BF16)

    def project(rows, h):
        return _dot(h, wq_ref[...])

    def finish(rows, z):
        q = _apply_rope(z, a_ref[rows, :], b_ref[rows, :], c_ref[rows, :])
        q_ref[rows, :] = (q * (DIFF_HEAD_DIM ** -0.5 * LOG2_E)).astype(BF16)

    _staged_rows(x_ref.shape[0], [norm, project, finish])


def _q_proj(x, g, wq, tabs):
    t = x.shape[0]
    return pl.pallas_call(
        _q_kernel,
        grid=(t // ROW_TILE,),
        in_specs=[_rows(ROW_TILE, D_MODEL), _resident((1, D_MODEL)), _resident((D_MODEL, D_MODEL))]
                 + [_rows(ROW_TILE, LANES)] * 3,
        out_specs=_rows(ROW_TILE, D_MODEL),
        out_shape=jax.ShapeDtypeStruct((t, D_MODEL), BF16),
        compiler_params=_params(1),
        name="diff_q_proj",
    )(x, g, wq, *tabs)


def _diff_attn_kernel(lam_ref, q_ref, k_ref, v_ref, g_ref, o_ref,
                      qs_ref, vext_ref, m_ref, acc_ref, *, lambda_init):
    tq = q_ref.shape[0]
    qi = pl.program_id(2)
    q = q_ref[...]
    lane = lax.broadcasted_iota(jnp.int32, q.shape, 1)
    zero = jnp.zeros_like(q)
    qs_ref[0:tq, :] = jnp.where(lane < DIFF_HEAD_DIM, q, zero)
    qs_ref[tq:, :] = jnp.where(lane >= DIFF_HEAD_DIM, q, zero)
    m_ref[...] = jnp.full(m_ref.shape, -jnp.inf, F32)
    acc_ref[...] = jnp.zeros(acc_ref.shape, F32)

    @pl.when(qi == 0)
    def _():
        vext_ref[:, 0:DIFF_V_DIM] = v_ref[...]
        vext_ref[:, DIFF_V_DIM:] = jnp.ones(v_ref.shape, BF16)

    n_chunks = 2 * tq // ATTN_ROW_CHUNK

    def chunk_rows(c):
        return slice(c * ATTN_ROW_CHUNK, (c + 1) * ATTN_ROW_CHUNK)

    def run_blocks(blocks):
        items = [(j, causal, c) for j, causal in blocks for c in range(n_chunks)]

        def start(j):
            return pl.multiple_of(j * tq, tq)

        def scores(item):
            j, _, c = item
            return _dot_nt(qs_ref[chunk_rows(c), :], k_ref[pl.ds(start(j), tq), :])

        def softmax(item, s):
            _, causal, c = item
            rows = chunk_rows(c)
            if causal:
                row = lax.broadcasted_iota(jnp.int32, s.shape, 0) + (c * ATTN_ROW_CHUNK) % tq
                col = lax.broadcasted_iota(jnp.int32, s.shape, 1)
                s = jnp.where(col <= row, s, -jnp.inf)
            m_prev = m_ref[rows, :]
            m_new = jnp.maximum(m_prev, jnp.max(s, axis=-1, keepdims=True))
            alpha = jnp.exp2(m_prev - m_new)
            p = jnp.exp2((s - jnp.tile(m_new, (1, tq // LANES))).astype(BF16))
            m_ref[rows, :] = m_new
            return alpha, p

        def values(item, alpha, p):
            j, _, c = item
            rows = chunk_rows(c)
            acc_ref[rows, :] = (jnp.tile(alpha, (1, 2)) * acc_ref[rows, :]
                                + _dot(p, vext_ref[pl.ds(start(j), tq), :]))

        s_vals = {i: scores(items[i]) for i in range(min(ATTN_LOOKAHEAD, len(items)))}
        pending = None
        for i, item in enumerate(items):
            alpha_p = softmax(item, s_vals.pop(i))
            if i + ATTN_LOOKAHEAD < len(items):
                s_vals[i + ATTN_LOOKAHEAD] = scores(items[i + ATTN_LOOKAHEAD])
            if pending is not None:
                values(items[i - 1], *pending)
            pending = alpha_p
        values(items[-1], *pending)

    def block_pair(i, carry):
        run_blocks([(2 * i, False), (2 * i + 1, False)])
        return carry

    lax.fori_loop(0, qi // 2, block_pair, 0)

    @pl.when(qi % 2 == 0)
    def _():
        run_blocks([(qi, True)])

    @pl.when(qi % 2 == 1)
    def _():
        run_blocks([(qi - 1, False), (qi, True)])

    lam_rows = lam_ref[...]
    lam = (jnp.exp(jnp.sum(lam_rows[0:1] * lam_rows[1:2], axis=-1, keepdims=True))
           - jnp.exp(jnp.sum(lam_rows[2:3] * lam_rows[3:4], axis=-1, keepdims=True))
           + lambda_init)
    o = acc_ref[:, 0:DIFF_V_DIM] / acc_ref[:, DIFF_V_DIM:]
    o = o[:tq] - lam * o[tq:]
    o = _rms(o, g_ref[...], SUBLN_EPS) * (1.0 - lambda_init)
    o_ref[...] = o.astype(BF16)


def _diff_attn(q, k, v, lam_rows, subln_g, lambda_init, batch, seq):
    t = q.shape[0]
    tq = ATTN_TILE
    nq = seq // tq
    q_spec = pl.BlockSpec((tq, DIFF_V_DIM), lambda b, h, i: (b * nq + i, h))
    kv_spec = pl.BlockSpec((seq, DIFF_V_DIM), lambda b, h, i: (b, h))
    return pl.pallas_call(
        functools.partial(_diff_attn_kernel, lambda_init=lambda_init),
        grid=(batch, DIFF_HEADS, nq),
        in_specs=[_resident((8, LANES)), q_spec, kv_spec, kv_spec, _resident((1, DIFF_V_DIM))],
        out_specs=q_spec,
        out_shape=jax.ShapeDtypeStruct((t, D_MODEL), BF16),
        scratch_shapes=[pltpu.VMEM((2 * tq, DIFF_V_DIM), BF16),
                        pltpu.VMEM((seq, 2 * DIFF_V_DIM), BF16),
                        pltpu.VMEM((2 * tq, LANES), F32),
                        pltpu.VMEM((2 * tq, 2 * DIFF_V_DIM), F32)],
        compiler_params=_params(3),
        name="diff_attn",
    )(lam_rows, q, k, v, subln_g)


def _wo_kernel(o_ref, w_ref, x_ref, y_ref):
    y_ref[...] = x_ref[...] + _dot(o_ref[...], w_ref[...])


def _out_proj(o, w, x):
    t = x.shape[0]
    return pl.pallas_call(
        _wo_kernel,
        grid=(t // ROW_TILE,),
        in_specs=[_rows(ROW_TILE, D_MODEL), _resident((D_MODEL, D_MODEL)), _rows(ROW_TILE, D_MODEL)],
        out_specs=_rows(ROW_TILE, D_MODEL),
        out_shape=jax.ShapeDtypeStruct((t, D_MODEL), F32),
        compiler_params=_params(1),
        name="diff_out_proj",
    )(o, w, x)


def kernel(x, mem, positions, norm_mix, norm_mem, norm_ffn, norm_final, conv_w_pw1, conv_b_pw1, conv_w_dw, conv_b_dw, conv_ln_g, conv_ln_b, conv_w_pw2, conv_b_pw2, kv_norm, w_k_shared, w_v_shared, diff_w_q, diff_lambda_q1, diff_lambda_k1, diff_lambda_q2, diff_lambda_k2, diff_subln_g, diff_w_o, mem_w_q, mem_w_k, mem_w_v, mem_w_o, ffn_w_gate, ffn_w_up, ffn_w_down):
    batch, seq, d = x.shape
    t = batch * seq
    row = lambda v: v.reshape(1, -1).astype(F32)
    bf = lambda w: w.astype(BF16)

    xs = x.reshape(t, d)
    tabs = _rope_tables(positions)
    mem_k, mem_v = _memkv(mem.reshape(-1, d), bf(mem_w_k), bf(mem_w_v))
    mem_k = mem_k.reshape(DEPTH, batch, -1, d)
    mem_v = mem_v.reshape(DEPTH, batch, -1, d)

    k_sh = v_sh = None
    for i in range(DEPTH):
        if i < N_A_LAYERS:
            u = _pw1(xs, row(norm_mix[i]), bf(conv_w_pw1[i]), row(conv_b_pw1[i]))
            wdw = jnp.repeat(conv_w_dw[i].astype(F32), SUBLANES, axis=0)
            xs = _conv(u, xs, wdw, row(conv_b_dw[i]), row(conv_ln_g[i]), row(conv_ln_b[i]),
                       bf(conv_w_pw2[i]), row(conv_b_pw2[i]), batch, seq)
        else:
            b = i - N_A_LAYERS
            if b == 0:
                k_sh, v_sh = _shared_kv(xs, row(kv_norm), bf(w_k_shared), bf(w_v_shared), tabs)
            lambda_init = 0.8 - 0.6 * math.exp(-0.3 * i)
            q = _q_proj(xs, row(norm_mix[i]), bf(diff_w_q[b]), tabs)
            lam_rows = jnp.stack([diff_lambda_q1[b], diff_lambda_k1[b],
                                  diff_lambda_q2[b], diff_lambda_k2[b]]).astype(F32)
            lam_rows = jnp.pad(lam_rows, ((0, 4), (0, LANES - DIFF_HEAD_DIM)))
            o = _diff_attn(q, k_sh, v_sh, lam_rows, row(diff_subln_g[b]), lambda_init, batch, seq)
            xs = _out_proj(o, bf(diff_w_o[b]), xs)
        xs = _memattn(xs, row(norm_mem[i]), bf(mem_w_q[i]), mem_k, mem_v, bf(mem_w_o[i]),
                      i, batch, seq)
        xs = _ffn(xs, row(norm_ffn[i]), bf(ffn_w_gate[i]), bf(ffn_w_up[i]), bf(ffn_w_down[i]),
                  row(norm_final), final_norm=(i == DEPTH - 1))
    return xs.reshape(batch, seq, d)
```

```python
import functools
import math

import jax
import jax.numpy as jnp
from jax import lax
from jax.experimental import pallas as pl
from jax.experimental.pallas import tpu as pltpu

D_MODEL = 1024
DEPTH = 4
N_A_LAYERS = DEPTH // 2
CONV_WIDTH = 31
DIFF_HEADS = 8
DIFF_HEAD_DIM = 64
DIFF_V_DIM = 2 * DIFF_HEAD_DIM
ROT_DIM = DIFF_HEAD_DIM // 4
ROPE_THETA = 500000.0
MEM_HEADS = 4
MEM_HEAD_DIM = D_MODEL // MEM_HEADS
RMS_EPS = 1e-6
LN_EPS = 1e-5
SUBLN_EPS = 1e-5
LOG2_E = math.log2(math.e)

LANES = 128
SUBLANES = 8
BF16_ROWS = 16
CONV_HALO = 32
ROW_TILE = 1024
SUB_TILE = 128
ATTN_TILE = 512
ATTN_LOOKAHEAD = 2
ATTN_ROW_CHUNK = 512
FFN_CHUNK = 256
VMEM_LIMIT = 56 * 1024 * 1024

BF16 = jnp.bfloat16
F32 = jnp.float32


def _dot(a, b):
    return jnp.dot(a, b, preferred_element_type=F32)


def _dot_nt(a, b):
    return lax.dot_general(a, b, (((1,), (1,)), ((), ())), preferred_element_type=F32)


def _rms(x, g, eps):
    return x * lax.rsqrt(jnp.mean(x * x, axis=-1, keepdims=True) + eps) * g


def _staged_rows(n_rows, stages):
    n = n_rows // SUB_TILE
    depth = len(stages)
    live = {}
    for step in range(n + depth - 1):
        for s, stage in enumerate(stages):
            c = step - s
            if 0 <= c < n:
                rows = slice(c * SUB_TILE, (c + 1) * SUB_TILE)
                live[c] = stage(rows) if s == 0 else stage(rows, live[c])


def _params(n_grid_dims):
    return pltpu.CompilerParams(
        dimension_semantics=("arbitrary",) * n_grid_dims, vmem_limit_bytes=VMEM_LIMIT)


def _resident(shape):
    zeros = (0,) * len(shape)
    return pl.BlockSpec(shape, lambda *_: zeros, pipeline_mode=pl.Buffered(1))


def _layer_block(layer, shape):
    zeros = (0,) * len(shape)
    return pl.BlockSpec((None,) + tuple(shape), lambda *_: (layer,) + zeros,
                        pipeline_mode=pl.Buffered(1))


def _rows(tile, width):
    return pl.BlockSpec((tile, width), lambda i: (i, 0))


def _rope_table_kernel(pos_ref, invf_ref, a_ref, b_ref, c_ref):
    ang = pos_ref[...].astype(F32) * invf_ref[...]
    cos = jnp.cos(ang)
    sin = jnp.sin(ang)
    lane = lax.broadcasted_iota(jnp.int32, ang.shape, 1) % DIFF_HEAD_DIM
    half = ROT_DIM // 2
    a_ref[...] = jnp.where(lane < ROT_DIM, cos, 1.0)
    b_ref[...] = jnp.where(lane < half, -sin, 0.0)
    c_ref[...] = jnp.where((lane >= half) & (lane < ROT_DIM), sin, 0.0)


def _rope_tables(positions):
    t = positions.size
    half = ROT_DIM // 2
    inv_freq = ROPE_THETA ** (-jnp.arange(0, ROT_DIM, 2, dtype=F32) / ROT_DIM)
    lane = jnp.arange(LANES) % DIFF_HEAD_DIM
    invf = jnp.where(lane < ROT_DIM, inv_freq[lane % half], 0.0).reshape(1, LANES).astype(F32)
    tile = 2048
    out = jax.ShapeDtypeStruct((t, LANES), F32)
    return pl.pallas_call(
        _rope_table_kernel,
        grid=(t // tile,),
        in_specs=[pl.BlockSpec((tile, 1), lambda i: (i, 0)), _resident((1, LANES))],
        out_specs=[_rows(tile, LANES)] * 3,
        out_shape=[out] * 3,
        compiler_params=_params(1),
        name="rope_tables",
    )(positions.reshape(t, 1), invf)


def _apply_rope(z, a, b, c):
    half = ROT_DIM // 2
    cols = []
    for j in range(z.shape[1] // LANES):
        zj = z[:, j * LANES:(j + 1) * LANES]
        cols.append(zj * a + pltpu.roll(zj, LANES - half, 1) * b + pltpu.roll(zj, half, 1) * c)
    return jnp.concatenate(cols, axis=1)


def _pw1_kernel(x_ref, g_ref, w_ref, b_ref, u_ref):
    h = _rms(x_ref[...], g_ref[...], RMS_EPS).astype(BF16)
    z = _dot(h, w_ref[...]) + b_ref[...]
    u_ref[...] = z[:, :D_MODEL] * jax.nn.sigmoid(z[:, D_MODEL:])


def _pw1(x, g, w, b, layer):
    t = x.shape[0]
    return pl.pallas_call(
        _pw1_kernel,
        grid=(t // ROW_TILE,),
        in_specs=[_rows(ROW_TILE, D_MODEL), _resident((1, D_MODEL)),
                  _layer_block(layer, (D_MODEL, 2 * D_MODEL)), _resident((1, 2 * D_MODEL))],
        out_specs=_rows(ROW_TILE, D_MODEL),
        out_shape=jax.ShapeDtypeStruct((t, D_MODEL), F32),
        compiler_params=_params(1),
        name="conv_pw1_glu",
    )(x, g, w, b)


CONV_ROW_CHUNK = 64


def _conv_kernel(u_ref, halo_ref, x_ref, wdw_ref, bdw_ref, lng_ref, lnb_ref, w2_ref, b2_ref,
                 o_ref, ext_ref, y_ref, win_ref):
    tile = u_ref.shape[0]
    first = pl.program_id(1) == 0
    ext_ref[0:CONV_HALO, :] = jnp.where(first, 0.0, halo_ref[...])
    ext_ref[CONV_HALO:, :] = u_ref[...]
    lead = CONV_HALO - (CONV_WIDTH - 1)

    def row_chunk(r, carry):
        r0 = pl.multiple_of(r * CONV_ROW_CHUNK, CONV_ROW_CHUNK)
        for c in range(D_MODEL // LANES):
            cs = slice(c * LANES, (c + 1) * LANES)
            win_ref[c] = ext_ref[pl.ds(r0, CONV_ROW_CHUNK + CONV_HALO), cs]
        for c in range(D_MODEL // LANES):
            cs = slice(c * LANES, (c + 1) * LANES)
            acc = jnp.zeros((CONV_ROW_CHUNK, LANES), F32)
            for j in range(CONV_WIDTH):
                w16 = wdw_ref[j * BF16_ROWS:(j + 1) * BF16_ROWS, cs]
                w_rows = jnp.concatenate([w16] * (CONV_ROW_CHUNK // BF16_ROWS), axis=0)
                u_rows = win_ref[c, lead + j:lead + j + CONV_ROW_CHUNK, :]
                acc = acc + u_rows.astype(BF16).astype(F32) * w_rows.astype(F32)
            y_ref[pl.ds(r0, CONV_ROW_CHUNK), cs] = acc
        return carry

    lax.fori_loop(0, tile // CONV_ROW_CHUNK, row_chunk, 0)

    y = y_ref[...] + bdw_ref[...]
    mu = jnp.mean(y, axis=-1, keepdims=True)
    yc = y - mu
    yn = yc * lax.rsqrt(jnp.mean(yc * yc, axis=-1, keepdims=True) + LN_EPS)
    yn = yn * lng_ref[...] + lnb_ref[...]
    act = (yn * jax.nn.sigmoid(yn)).astype(BF16)
    o_ref[...] = x_ref[...] + _dot(act, w2_ref[...]) + b2_ref[...]


def _conv(u, x, wdw, bdw, lng, lnb, w2, b2, layer, batch, seq):
    t = u.shape[0]
    tiles = seq // ROW_TILE
    halo_per_tile = ROW_TILE // CONV_HALO
    row_spec = pl.BlockSpec((ROW_TILE, D_MODEL), lambda b, i: (b * tiles + i, 0))
    halo_spec = pl.BlockSpec(
        (CONV_HALO, D_MODEL),
        lambda b, i: (jnp.maximum((b * tiles + i) * halo_per_tile - 1, 0), 0))
    return pl.pallas_call(
        _conv_kernel,
        grid=(batch, tiles),
        in_specs=[row_spec, halo_spec, row_spec,
                  _resident((CONV_WIDTH * BF16_ROWS, D_MODEL)), _resident((1, D_MODEL)),
                  _resident((1, D_MODEL)),
                  _resident((1, D_MODEL)), _layer_block(layer, (D_MODEL, D_MODEL)),
                  _resident((1, D_MODEL))],
        out_specs=row_spec,
        out_shape=jax.ShapeDtypeStruct((t, D_MODEL), F32),
        scratch_shapes=[pltpu.VMEM((ROW_TILE + CONV_HALO, D_MODEL), F32),
                        pltpu.VMEM((ROW_TILE, D_MODEL), F32),
                        pltpu.VMEM((D_MODEL // LANES, CONV_ROW_CHUNK + CONV_HALO, LANES), F32)],
        compiler_params=_params(2),
        name="conv_dw_ln_pw2",
    )(u, u, x, wdw, bdw, lng, lnb, w2, b2)


def _memkv_kernel(mem_ref, wk_ref, wv_ref, k_ref, v_ref):
    m = mem_ref[...].astype(BF16)
    k_ref[0] = _dot(m, wk_ref[0]).astype(BF16)
    v_ref[0] = _dot(m, wv_ref[0]).astype(BF16)


def _memkv(mem2d, wk, wv):
    rows = mem2d.shape[0]
    wspec = pl.BlockSpec((1, D_MODEL, D_MODEL), lambda i: (i, 0, 0))
    ospec = pl.BlockSpec((1, rows, D_MODEL), lambda i: (i, 0, 0))
    out = jax.ShapeDtypeStruct((DEPTH, rows, D_MODEL), BF16)
    return pl.pallas_call(
        _memkv_kernel,
        grid=(DEPTH,),
        in_specs=[_resident((rows, D_MODEL)), wspec, wspec],
        out_specs=[ospec, ospec],
        out_shape=[out, out],
        compiler_params=_params(1),
        name="mem_kv_proj",
    )(mem2d, wk, wv)


def _memattn_kernel(x_ref, g_ref, wq_ref, k_ref, v_ref, wo_ref, o_ref):
    x = x_ref[...]
    h = _rms(x, g_ref[...], RMS_EPS).astype(BF16)
    q = (_dot(h, wq_ref[...]) * (MEM_HEAD_DIM ** -0.5)).astype(BF16)
    heads = []
    for hd in range(MEM_HEADS):
        cs = slice(hd * MEM_HEAD_DIM, (hd + 1) * MEM_HEAD_DIM)
        s = _dot_nt(q[:, cs], k_ref[0, 0, :, cs])
        e = jnp.exp(s - jnp.max(s, axis=-1, keepdims=True))
        p = e / jnp.sum(e, axis=-1, keepdims=True)
        heads.append(_dot(p.astype(BF16), v_ref[0, 0, :, cs]).astype(BF16))
    o = jnp.concatenate(heads, axis=-1)
    o_ref[...] = x + _dot(o, wo_ref[...])


def _memattn(x, g, wq, k_all, v_all, wo, layer, batch, seq):
    t = x.shape[0]
    tiles = seq // ROW_TILE
    m = k_all.shape[2]
    row_spec = pl.BlockSpec((ROW_TILE, D_MODEL), lambda b, i: (b * tiles + i, 0))
    kv_spec = pl.BlockSpec((1, 1, m, D_MODEL), lambda b, i: (layer, b, 0, 0))
    return pl.pallas_call(
        _memattn_kernel,
        grid=(batch, tiles),
        in_specs=[row_spec, _resident((1, D_MODEL)), _layer_block(layer, (D_MODEL, D_MODEL)),
                  kv_spec, kv_spec, _layer_block(layer, (D_MODEL, D_MODEL))],
        out_specs=row_spec,
        out_shape=jax.ShapeDtypeStruct((t, D_MODEL), F32),
        compiler_params=_params(2),
        name="mem_cross_attn",
    )(x, g, wq, k_all, v_all, wo)


def _ffn_kernel(x_ref, g_ref, wg_ref, wu_ref, wd_ref, gf_ref, o_ref, act_ref, *, final_norm):
    x = x_ref[...]
    h = _rms(x, g_ref[...], RMS_EPS).astype(BF16)
    hidden = wg_ref.shape[1]
    for c in range(hidden // FFN_CHUNK):
        cs = slice(c * FFN_CHUNK, (c + 1) * FFN_CHUNK)
        gate = _dot(h, wg_ref[:, cs])
        up = _dot(h, wu_ref[:, cs])
        act_ref[:, cs] = (gate * jax.nn.sigmoid(gate) * up).astype(BF16)
    y = x + _dot(act_ref[...], wd_ref[...])
    if final_norm:
        y = _rms(y, gf_ref[...], RMS_EPS)
    o_ref[...] = y


def _ffn(x, g, wg, wu, wd, g_final, final_norm, layer):
    t = x.shape[0]
    hidden = wg.shape[2]
    return pl.pallas_call(
        functools.partial(_ffn_kernel, final_norm=final_norm),
        grid=(t // ROW_TILE,),
        in_specs=[_rows(ROW_TILE, D_MODEL), _resident((1, D_MODEL)),
                  _layer_block(layer, (D_MODEL, hidden)), _layer_block(layer, (D_MODEL, hidden)),
                  _layer_block(layer, (hidden, D_MODEL)), _resident((1, D_MODEL))],
        out_specs=_rows(ROW_TILE, D_MODEL),
        out_shape=jax.ShapeDtypeStruct((t, D_MODEL), F32),
        scratch_shapes=[pltpu.VMEM((ROW_TILE, hidden), BF16)],
        compiler_params=_params(1),
        name="swiglu_ffn",
    )(x, g, wg, wu, wd, g_final)


def _kv_kernel(x_ref, g_ref, wk_ref, wv_ref, a_ref, b_ref, c_ref, k_ref, v_ref):
    h = _rms(x_ref[...], g_ref[...], RMS_EPS).astype(BF16)
    k = _apply_rope(_dot(h, wk_ref[...]), a_ref[...], b_ref[...], c_ref[...])
    k_ref[...] = k.astype(BF16)
    v_ref[...] = _dot(h, wv_ref[...]).astype(BF16)


def _shared_kv(x, g, wk, wv, tabs):
    t = x.shape[0]
    out = jax.ShapeDtypeStruct((t, D_MODEL), BF16)
    return pl.pallas_call(
        _kv_kernel,
        grid=(t // ROW_TILE,),
        in_specs=[_rows(ROW_TILE, D_MODEL), _resident((1, D_MODEL)),
                  _resident((D_MODEL, D_MODEL)), _resident((D_MODEL, D_MODEL))]
                 + [_rows(ROW_TILE, LANES)] * 3,
        out_specs=[_rows(ROW_TILE, D_MODEL)] * 2,
        out_shape=[out, out],
        compiler_params=_params(1),
        name="shared_kv_proj",
    )(x, g, wk, wv, *tabs)


def _q_kernel(x_ref, g_ref, wq_ref, a_ref, b_ref, c_ref, q_ref):
    def norm(rows):
        return _rms(x_ref[rows, :], g_ref[...], RMS_EPS).astype(BF16)

    def project(rows, h):
        return _dot(h, wq_ref[...])

    def finish(rows, z):
        q = _apply_rope(z, a_ref[rows, :], b_ref[rows, :], c_ref[rows, :])
        q_ref[rows, :] = (q * (DIFF_HEAD_DIM ** -0.5 * LOG2_E)).astype(BF16)

    _staged_rows(x_ref.shape[0], [norm, project, finish])


def _q_proj(x, g, wq, tabs, layer):
    t = x.shape[0]
    return pl.pallas_call(
        _q_kernel,
        grid=(t // ROW_TILE,),
        in_specs=[_rows(ROW_TILE, D_MODEL), _resident((1, D_MODEL)),
                  _layer_block(layer, (D_MODEL, D_MODEL))] + [_rows(ROW_TILE, LANES)] * 3,
        out_specs=_rows(ROW_TILE, D_MODEL),
        out_shape=jax.ShapeDtypeStruct((t, D_MODEL), BF16),
        compiler_params=_params(1),
        name="diff_q_proj",
    )(x, g, wq, *tabs)


def _diff_attn_kernel(lam_ref, q_ref, k_ref, v_ref, g_ref, o_ref,
                      qs_ref, vext_ref, m_ref, acc_ref, *, lambda_init):
    tq = q_ref.shape[0]
    qi = pl.program_id(2)
    q = q_ref[...]
    lane = lax.broadcasted_iota(jnp.int32, q.shape, 1)
    zero = jnp.zeros_like(q)
    qs_ref[0:tq, :] = jnp.where(lane < DIFF_HEAD_DIM, q, zero)
    qs_ref[tq:, :] = jnp.where(lane >= DIFF_HEAD_DIM, q, zero)
    m_ref[...] = jnp.full(m_ref.shape, -jnp.inf, F32)
    acc_ref[...] = jnp.zeros(acc_ref.shape, F32)

    @pl.when(qi == 0)
    def _():
        vext_ref[:, 0:DIFF_V_DIM] = v_ref[...]
        vext_ref[:, DIFF_V_DIM:] = jnp.ones(v_ref.shape, BF16)

    n_chunks = 2 * tq // ATTN_ROW_CHUNK

    def chunk_rows(c):
        return slice(c * ATTN_ROW_CHUNK, (c + 1) * ATTN_ROW_CHUNK)

    def run_blocks(blocks):
        items = [(j, causal, c) for j, causal in blocks for c in range(n_chunks)]

        def start(j):
            return pl.multiple_of(j * tq, tq)

        def scores(item):
            j, _, c = item
            return _dot_nt(qs_ref[chunk_rows(c), :], k_ref[pl.ds(start(j), tq), :])

        def softmax(item, s):
            _, causal, c = item
            rows = chunk_rows(c)
            if causal:
                row = lax.broadcasted_iota(jnp.int32, s.shape, 0) + (c * ATTN_ROW_CHUNK) % tq
                col = lax.broadcasted_iota(jnp.int32, s.shape, 1)
                s = jnp.where(col <= row, s, -jnp.inf)
            m_prev = m_ref[rows, :]
            m_new = jnp.maximum(m_prev, jnp.max(s, axis=-1, keepdims=True))
            alpha = jnp.exp2(m_prev - m_new)
            p = jnp.exp2((s - jnp.tile(m_new, (1, tq // LANES))).astype(BF16))
            m_ref[rows, :] = m_new
            return alpha, p

        def values(item, alpha, p):
            j, _, c = item
            rows = chunk_rows(c)
            acc_ref[rows, :] = (jnp.tile(alpha, (1, 2)) * acc_ref[rows, :]
                                + _dot(p, vext_ref[pl.ds(start(j), tq), :]))

        s_vals = {i: scores(items[i]) for i in range(min(ATTN_LOOKAHEAD, len(items)))}
        pending = None
        for i, item in enumerate(items):
            alpha_p = softmax(item, s_vals.pop(i))
            if i + ATTN_LOOKAHEAD < len(items):
                s_vals[i + ATTN_LOOKAHEAD] = scores(items[i + ATTN_LOOKAHEAD])
            if pending is not None:
                values(items[i - 1], *pending)
            pending = alpha_p
        values(items[-1], *pending)

    def block_pair(i, carry):
        run_blocks([(2 * i, False), (2 * i + 1, False)])
        return carry

    lax.fori_loop(0, qi // 2, block_pair, 0)

    @pl.when(qi % 2 == 0)
    def _():
        run_blocks([(qi, True)])

    @pl.when(qi % 2 == 1)
    def _():
        run_blocks([(qi - 1, False), (qi, True)])

    lam_rows = lam_ref[...]
    lam = (jnp.exp(jnp.sum(lam_rows[0:1] * lam_rows[1:2], axis=-1, keepdims=True))
           - jnp.exp(jnp.sum(lam_rows[2:3] * lam_rows[3:4], axis=-1, keepdims=True))
           + lambda_init)
    o = acc_ref[:, 0:DIFF_V_DIM] / acc_ref[:, DIFF_V_DIM:]
    o = o[:tq] - lam * o[tq:]
    o = _rms(o, g_ref[...], SUBLN_EPS) * (1.0 - lambda_init)
    o_ref[...] = o.astype(BF16)


def _diff_attn(q, k, v, lam_rows, subln_g, lambda_init, batch, seq):
    t = q.shape[0]
    tq = ATTN_TILE
    nq = seq // tq
    q_spec = pl.BlockSpec((tq, DIFF_V_DIM), lambda b, h, i: (b * nq + i, h))
    kv_spec = pl.BlockSpec((seq, DIFF_V_DIM), lambda b, h, i: (b, h))
    return pl.pallas_call(
        functools.partial(_diff_attn_kernel, lambda_init=lambda_init),
        grid=(batch, DIFF_HEADS, nq),
        in_specs=[_resident((8, LANES)), q_spec, kv_spec, kv_spec, _resident((1, DIFF_V_DIM))],
        out_specs=q_spec,
        out_shape=jax.ShapeDtypeStruct((t, D_MODEL), BF16),
        scratch_shapes=[pltpu.VMEM((2 * tq, DIFF_V_DIM), BF16),
                        pltpu.VMEM((seq, 2 * DIFF_V_DIM), BF16),
                        pltpu.VMEM((2 * tq, LANES), F32),
                        pltpu.VMEM((2 * tq, 2 * DIFF_V_DIM), F32)],
        compiler_params=_params(3),
        name="diff_attn",
    )(lam_rows, q, k, v, subln_g)


def _wo_kernel(o_ref, w_ref, x_ref, y_ref):
    y_ref[...] = x_ref[...] + _dot(o_ref[...], w_ref[...])


def _out_proj(o, w, x, layer):
    t = x.shape[0]
    return pl.pallas_call(
        _wo_kernel,
        grid=(t // ROW_TILE,),
        in_specs=[_rows(ROW_TILE, D_MODEL), _layer_block(layer, (D_MODEL, D_MODEL)),
                  _rows(ROW_TILE, D_MODEL)],
        out_specs=_rows(ROW_TILE, D_MODEL),
        out_shape=jax.ShapeDtypeStruct((t, D_MODEL), F32),
        compiler_params=_params(1),
        name="diff_out_proj",
    )(o, w, x)


def kernel(x, mem, positions, norm_mix, norm_mem, norm_ffn, norm_final, conv_w_pw1, conv_b_pw1, conv_w_dw, conv_b_dw, conv_ln_g, conv_ln_b, conv_w_pw2, conv_b_pw2, kv_norm, w_k_shared, w_v_shared, diff_w_q, diff_lambda_q1, diff_lambda_k1, diff_lambda_q2, diff_lambda_k2, diff_subln_g, diff_w_o, mem_w_q, mem_w_k, mem_w_v, mem_w_o, ffn_w_gate, ffn_w_up, ffn_w_down):
    batch, seq, d = x.shape
    t = batch * seq
    row = lambda v: v.reshape(1, -1).astype(F32)
    bf = lambda w: w.astype(BF16)

    conv_w_pw1, conv_w_pw2, diff_w_q, diff_w_o = map(bf, (conv_w_pw1, conv_w_pw2, diff_w_q, diff_w_o))
    mem_w_q, mem_w_k, mem_w_v, mem_w_o = map(bf, (mem_w_q, mem_w_k, mem_w_v, mem_w_o))
    ffn_w_gate, ffn_w_up, ffn_w_down = map(bf, (ffn_w_gate, ffn_w_up, ffn_w_down))

    xs = x.reshape(t, d)
    tabs = _rope_tables(positions)
    mem_k, mem_v = _memkv(mem.reshape(-1, d), mem_w_k, mem_w_v)
    mem_k = mem_k.reshape(DEPTH, batch, -1, d)
    mem_v = mem_v.reshape(DEPTH, batch, -1, d)

    k_sh = v_sh = None
    for i in range(DEPTH):
        if i < N_A_LAYERS:
            u = _pw1(xs, row(norm_mix[i]), conv_w_pw1, row(conv_b_pw1[i]), i)
            wdw = jnp.repeat(conv_w_dw[i].astype(BF16), BF16_ROWS, axis=0)
            xs = _conv(u, xs, wdw, row(conv_b_dw[i]), row(conv_ln_g[i]), row(conv_ln_b[i]),
                       conv_w_pw2, row(conv_b_pw2[i]), i, batch, seq)
        else:
            b = i - N_A_LAYERS
            if b == 0:
                k_sh, v_sh = _shared_kv(xs, row(kv_norm), bf(w_k_shared), bf(w_v_shared), tabs)
            lambda_init = 0.8 - 0.6 * math.exp(-0.3 * i)
            q = _q_proj(xs, row(norm_mix[i]), diff_w_q, tabs, b)
            lam_rows = jnp.stack([diff_lambda_q1[b], diff_lambda_k1[b],
                                  diff_lambda_q2[b], diff_lambda_k2[b]]).astype(F32)
            lam_rows = jnp.pad(lam_rows, ((0, 4), (0, LANES - DIFF_HEAD_DIM)))
            o = _diff_attn(q, k_sh, v_sh, lam_rows, row(diff_subln_g[b]), lambda_init, batch, seq)
            xs = _out_proj(o, diff_w_o, xs, b)
        xs = _memattn(xs, row(norm_mem[i]), mem_w_q, mem_k, mem_v, mem_w_o, i, batch, seq)
        xs = _ffn(xs, row(norm_ffn[i]), ffn_w_gate, ffn_w_up, ffn_w_down,
                  row(norm_final), final_norm=(i == DEPTH - 1), layer=i)
    return xs.reshape(batch, seq, d)
```

```python
import functools
import math

import jax
import jax.numpy as jnp
from jax import lax
from jax.experimental import pallas as pl
from jax.experimental.pallas import tpu as pltpu

D_MODEL = 1024
DEPTH = 4
N_A_LAYERS = DEPTH // 2
CONV_WIDTH = 31
DIFF_HEADS = 8
DIFF_HEAD_DIM = 64
DIFF_V_DIM = 2 * DIFF_HEAD_DIM
ROT_DIM = DIFF_HEAD_DIM // 4
ROPE_THETA = 500000.0
MEM_HEADS = 4
MEM_HEAD_DIM = D_MODEL // MEM_HEADS
RMS_EPS = 1e-6
LN_EPS = 1e-5
SUBLN_EPS = 1e-5
LOG2_E = math.log2(math.e)

LANES = 128
SUBLANES = 8
BF16_ROWS = 16
CONV_HALO = 32
ROW_TILE = 1024
SUB_TILE = 128
ATTN_TILE = 2048
ATTN_KEY_TILE = 512
ATTN_LOOKAHEAD = 2
FFN_CHUNK = 256
VMEM_LIMIT = 56 * 1024 * 1024

BF16 = jnp.bfloat16
F32 = jnp.float32


def _dot(a, b):
    return jnp.dot(a, b, preferred_element_type=F32)


def _dot_nt(a, b):
    return lax.dot_general(a, b, (((1,), (1,)), ((), ())), preferred_element_type=F32)


def _rms(x, g, eps):
    return x * lax.rsqrt(jnp.mean(x * x, axis=-1, keepdims=True) + eps) * g


def _staged_rows(n_rows, stages):
    n = n_rows // SUB_TILE
    depth = len(stages)
    live = {}
    for step in range(n + depth - 1):
        for s, stage in enumerate(stages):
            c = step - s
            if 0 <= c < n:
                rows = slice(c * SUB_TILE, (c + 1) * SUB_TILE)
                live[c] = stage(rows) if s == 0 else stage(rows, live[c])


def _params(n_grid_dims):
    return pltpu.CompilerParams(
        dimension_semantics=("arbitrary",) * n_grid_dims, vmem_limit_bytes=VMEM_LIMIT)


def _resident(shape):
    zeros = (0,) * len(shape)
    return pl.BlockSpec(shape, lambda *_: zeros, pipeline_mode=pl.Buffered(1))


def _layer_block(layer, shape):
    zeros = (0,) * len(shape)
    return pl.BlockSpec((None,) + tuple(shape), lambda *_: (layer,) + zeros,
                        pipeline_mode=pl.Buffered(1))


def _rows(tile, width):
    return pl.BlockSpec((tile, width), lambda i: (i, 0))


def _rope_table_kernel(pos_ref, invf_ref, a_ref, b_ref, c_ref):
    ang = pos_ref[...].astype(F32) * invf_ref[...]
    cos = jnp.cos(ang)
    sin = jnp.sin(ang)
    lane = lax.broadcasted_iota(jnp.int32, ang.shape, 1) % DIFF_HEAD_DIM
    half = ROT_DIM // 2
    a_ref[...] = jnp.where(lane < ROT_DIM, cos, 1.0)
    b_ref[...] = jnp.where(lane < half, -sin, 0.0)
    c_ref[...] = jnp.where((lane >= half) & (lane < ROT_DIM), sin, 0.0)


def _rope_tables(positions):
    t = positions.size
    half = ROT_DIM // 2
    inv_freq = ROPE_THETA ** (-jnp.arange(0, ROT_DIM, 2, dtype=F32) / ROT_DIM)
    lane = jnp.arange(LANES) % DIFF_HEAD_DIM
    invf = jnp.where(lane < ROT_DIM, inv_freq[lane % half], 0.0).reshape(1, LANES).astype(F32)
    tile = 2048
    out = jax.ShapeDtypeStruct((t, LANES), F32)
    return pl.pallas_call(
        _rope_table_kernel,
        grid=(t // tile,),
        in_specs=[pl.BlockSpec((tile, 1), lambda i: (i, 0)), _resident((1, LANES))],
        out_specs=[_rows(tile, LANES)] * 3,
        out_shape=[out] * 3,
        compiler_params=_params(1),
        name="rope_tables",
    )(positions.reshape(t, 1), invf)


def _apply_rope(z, a, b, c):
    half = ROT_DIM // 2
    cols = []
    for j in range(z.shape[1] // LANES):
        zj = z[:, j * LANES:(j + 1) * LANES]
        cols.append(zj * a + pltpu.roll(zj, LANES - half, 1) * b + pltpu.roll(zj, half, 1) * c)
    return jnp.concatenate(cols, axis=1)


def _pw1_kernel(x_ref, g_ref, w_ref, b_ref, u_ref):
    h = _rms(x_ref[...], g_ref[...], RMS_EPS).astype(BF16)
    z = _dot(h, w_ref[...]) + b_ref[...]
    u_ref[...] = z[:, :D_MODEL] * jax.nn.sigmoid(z[:, D_MODEL:])


def _pw1(x, g, w, b, layer):
    t = x.shape[0]
    return pl.pallas_call(
        _pw1_kernel,
        grid=(t // ROW_TILE,),
        in_specs=[_rows(ROW_TILE, D_MODEL), _resident((1, D_MODEL)),
                  _layer_block(layer, (D_MODEL, 2 * D_MODEL)), _resident((1, 2 * D_MODEL))],
        out_specs=_rows(ROW_TILE, D_MODEL),
        out_shape=jax.ShapeDtypeStruct((t, D_MODEL), F32),
        compiler_params=_params(1),
        name="conv_pw1_glu",
    )(x, g, w, b)


CONV_ROW_CHUNK = 64


def _conv_kernel(u_ref, halo_ref, x_ref, wdw_ref, bdw_ref, lng_ref, lnb_ref, w2_ref, b2_ref,
                 o_ref, ext_ref, y_ref, win_ref):
    tile = u_ref.shape[0]
    first = pl.program_id(1) == 0
    ext_ref[0:CONV_HALO, :] = jnp.where(first, 0.0, halo_ref[...])
    ext_ref[CONV_HALO:, :] = u_ref[...]
    lead = CONV_HALO - (CONV_WIDTH - 1)

    def row_chunk(r, carry):
        r0 = pl.multiple_of(r * CONV_ROW_CHUNK, CONV_ROW_CHUNK)
        for c in range(D_MODEL // LANES):
            cs = slice(c * LANES, (c + 1) * LANES)
            win_ref[c] = ext_ref[pl.ds(r0, CONV_ROW_CHUNK + CONV_HALO), cs]
        for c in range(D_MODEL // LANES):
            cs = slice(c * LANES, (c + 1) * LANES)
            acc = jnp.zeros((CONV_ROW_CHUNK, LANES), F32)
            for j in range(CONV_WIDTH):
                w16 = wdw_ref[j * BF16_ROWS:(j + 1) * BF16_ROWS, cs]
                w_rows = jnp.concatenate([w16] * (CONV_ROW_CHUNK // BF16_ROWS), axis=0)
                u_rows = win_ref[c, lead + j:lead + j + CONV_ROW_CHUNK, :]
                acc = acc + u_rows.astype(BF16).astype(F32) * w_rows.astype(F32)
            y_ref[pl.ds(r0, CONV_ROW_CHUNK), cs] = acc
        return carry

    lax.fori_loop(0, tile // CONV_ROW_CHUNK, row_chunk, 0)

    y = y_ref[...] + bdw_ref[...]
    mu = jnp.mean(y, axis=-1, keepdims=True)
    yc = y - mu
    yn = yc * lax.rsqrt(jnp.mean(yc * yc, axis=-1, keepdims=True) + LN_EPS)
    yn = yn * lng_ref[...] + lnb_ref[...]
    act = (yn * jax.nn.sigmoid(yn)).astype(BF16)
    o_ref[...] = x_ref[...] + _dot(act, w2_ref[...]) + b2_ref[...]


def _conv(u, x, wdw, bdw, lng, lnb, w2, b2, layer, batch, seq):
    t = u.shape[0]
    tiles = seq // ROW_TILE
    halo_per_tile = ROW_TILE // CONV_HALO
    row_spec = pl.BlockSpec((ROW_TILE, D_MODEL), lambda b, i: (b * tiles + i, 0))
    halo_spec = pl.BlockSpec(
        (CONV_HALO, D_MODEL),
        lambda b, i: (jnp.maximum((b * tiles + i) * halo_per_tile - 1, 0), 0))
    return pl.pallas_call(
        _conv_kernel,
        grid=(batch, tiles),
        in_specs=[row_spec, halo_spec, row_spec,
                  _resident((CONV_WIDTH * BF16_ROWS, D_MODEL)), _resident((1, D_MODEL)),
                  _resident((1, D_MODEL)),
                  _resident((1, D_MODEL)), _layer_block(layer, (D_MODEL, D_MODEL)),
                  _resident((1, D_MODEL))],
        out_specs=row_spec,
        out_shape=jax.ShapeDtypeStruct((t, D_MODEL), F32),
        scratch_shapes=[pltpu.VMEM((ROW_TILE + CONV_HALO, D_MODEL), F32),
                        pltpu.VMEM((ROW_TILE, D_MODEL), F32),
                        pltpu.VMEM((D_MODEL // LANES, CONV_ROW_CHUNK + CONV_HALO, LANES), F32)],
        compiler_params=_params(2),
        name="conv_dw_ln_pw2",
    )(u, u, x, wdw, bdw, lng, lnb, w2, b2)


def _memkv_kernel(mem_ref, wk_ref, wv_ref, k_ref, v_ref):
    m = mem_ref[...].astype(BF16)
    k_ref[0] = _dot(m, wk_ref[0]).astype(BF16)
    v_ref[0] = _dot(m, wv_ref[0]).astype(BF16)


def _memkv(mem2d, wk, wv):
    rows = mem2d.shape[0]
    wspec = pl.BlockSpec((1, D_MODEL, D_MODEL), lambda i: (i, 0, 0))
    ospec = pl.BlockSpec((1, rows, D_MODEL), lambda i: (i, 0, 0))
    out = jax.ShapeDtypeStruct((DEPTH, rows, D_MODEL), BF16)
    return pl.pallas_call(
        _memkv_kernel,
        grid=(DEPTH,),
        in_specs=[_resident((rows, D_MODEL)), wspec, wspec],
        out_specs=[ospec, ospec],
        out_shape=[out, out],
        compiler_params=_params(1),
        name="mem_kv_proj",
    )(mem2d, wk, wv)


def _memattn_kernel(x_ref, g_ref, wq_ref, k_ref, v_ref, wo_ref, o_ref):
    x = x_ref[...]
    h = _rms(x, g_ref[...], RMS_EPS).astype(BF16)
    q = (_dot(h, wq_ref[...]) * (MEM_HEAD_DIM ** -0.5)).astype(BF16)
    heads = []
    for hd in range(MEM_HEADS):
        cs = slice(hd * MEM_HEAD_DIM, (hd + 1) * MEM_HEAD_DIM)
        s = _dot_nt(q[:, cs], k_ref[0, 0, :, cs])
        e = jnp.exp(s - jnp.max(s, axis=-1, keepdims=True))
        p = e / jnp.sum(e, axis=-1, keepdims=True)
        heads.append(_dot(p.astype(BF16), v_ref[0, 0, :, cs]).astype(BF16))
    o = jnp.concatenate(heads, axis=-1)
    o_ref[...] = x + _dot(o, wo_ref[...])


def _memattn(x, g, wq, k_all, v_all, wo, layer, batch, seq):
    t = x.shape[0]
    tiles = seq // ROW_TILE
    m = k_all.shape[2]
    row_spec = pl.BlockSpec((ROW_TILE, D_MODEL), lambda b, i: (b * tiles + i, 0))
    kv_spec = pl.BlockSpec((1, 1, m, D_MODEL), lambda b, i: (layer, b, 0, 0))
    return pl.pallas_call(
        _memattn_kernel,
        grid=(batch, tiles),
        in_specs=[row_spec, _resident((1, D_MODEL)), _layer_block(layer, (D_MODEL, D_MODEL)),
                  kv_spec, kv_spec, _layer_block(layer, (D_MODEL, D_MODEL))],
        out_specs=row_spec,
        out_shape=jax.ShapeDtypeStruct((t, D_MODEL), F32),
        compiler_params=_params(2),
        name="mem_cross_attn",
    )(x, g, wq, k_all, v_all, wo)


def _ffn_kernel(x_ref, g_ref, wg_ref, wu_ref, wd_ref, gf_ref, o_ref, act_ref, *, final_norm):
    x = x_ref[...]
    h = _rms(x, g_ref[...], RMS_EPS).astype(BF16)
    hidden = wg_ref.shape[1]
    for c in range(hidden // FFN_CHUNK):
        cs = slice(c * FFN_CHUNK, (c + 1) * FFN_CHUNK)
        gate = _dot(h, wg_ref[:, cs])
        up = _dot(h, wu_ref[:, cs])
        act_ref[:, cs] = (gate * jax.nn.sigmoid(gate) * up).astype(BF16)
    y = x + _dot(act_ref[...], wd_ref[...])
    if final_norm:
        y = _rms(y, gf_ref[...], RMS_EPS)
    o_ref[...] = y


def _ffn(x, g, wg, wu, wd, g_final, final_norm, layer):
    t = x.shape[0]
    hidden = wg.shape[2]
    return pl.pallas_call(
        functools.partial(_ffn_kernel, final_norm=final_norm),
        grid=(t // ROW_TILE,),
        in_specs=[_rows(ROW_TILE, D_MODEL), _resident((1, D_MODEL)),
                  _layer_block(layer, (D_MODEL, hidden)), _layer_block(layer, (D_MODEL, hidden)),
                  _layer_block(layer, (hidden, D_MODEL)), _resident((1, D_MODEL))],
        out_specs=_rows(ROW_TILE, D_MODEL),
        out_shape=jax.ShapeDtypeStruct((t, D_MODEL), F32),
        scratch_shapes=[pltpu.VMEM((ROW_TILE, hidden), BF16)],
        compiler_params=_params(1),
        name="swiglu_ffn",
    )(x, g, wg, wu, wd, g_final)


def _kv_kernel(x_ref, g_ref, wk_ref, wv_ref, a_ref, b_ref, c_ref, k_ref, v_ref):
    h = _rms(x_ref[...], g_ref[...], RMS_EPS).astype(BF16)
    k = _apply_rope(_dot(h, wk_ref[...]), a_ref[...], b_ref[...], c_ref[...])
    k_ref[...] = k.astype(BF16)
    v_ref[...] = _dot(h, wv_ref[...]).astype(BF16)


def _shared_kv(x, g, wk, wv, tabs):
    t = x.shape[0]
    out = jax.ShapeDtypeStruct((t, D_MODEL), BF16)
    return pl.pallas_call(
        _kv_kernel,
        grid=(t // ROW_TILE,),
        in_specs=[_rows(ROW_TILE, D_MODEL), _resident((1, D_MODEL)),
                  _resident((D_MODEL, D_MODEL)), _resident((D_MODEL, D_MODEL))]
                 + [_rows(ROW_TILE, LANES)] * 3,
        out_specs=[_rows(ROW_TILE, D_MODEL)] * 2,
        out_shape=[out, out],
        compiler_params=_params(1),
        name="shared_kv_proj",
    )(x, g, wk, wv, *tabs)


def _q_kernel(x_ref, g_ref, wq_ref, a_ref, b_ref, c_ref, q_ref):
    def norm(rows):
        return _rms(x_ref[rows, :], g_ref[...], RMS_EPS).astype(BF16)

    def project(rows, h):
        return _dot(h, wq_ref[...])

    def finish(rows, z):
        q = _apply_rope(z, a_ref[rows, :], b_ref[rows, :], c_ref[rows, :])
        q_ref[rows, :] = (q * (DIFF_HEAD_DIM ** -0.5 * LOG2_E)).astype(BF16)

    _staged_rows(x_ref.shape[0], [norm, project, finish])


def _q_proj(x, g, wq, tabs, layer):
    t = x.shape[0]
    return pl.pallas_call(
        _q_kernel,
        grid=(t // ROW_TILE,),
        in_specs=[_rows(ROW_TILE, D_MODEL), _resident((1, D_MODEL)),
                  _layer_block(layer, (D_MODEL, D_MODEL))] + [_rows(ROW_TILE, LANES)] * 3,
        out_specs=_rows(ROW_TILE, D_MODEL),
        out_shape=jax.ShapeDtypeStruct((t, D_MODEL), BF16),
        compiler_params=_params(1),
        name="diff_q_proj",
    )(x, g, wq, *tabs)


def _diff_attn_kernel(lam_ref, q_ref, k_ref, v_ref, g_ref, o_ref,
                      qs_ref, vext_ref, m_ref, acc_ref, *, lambda_init):
    tq = q_ref.shape[0]
    tk = ATTN_KEY_TILE
    halves = tq // tk
    qi = pl.program_id(2)
    q = q_ref[...]
    lane = lax.broadcasted_iota(jnp.int32, q.shape, 1)
    zero = jnp.zeros_like(q)
    qs_ref[0:tq, :] = jnp.where(lane < DIFF_HEAD_DIM, q, zero)
    qs_ref[tq:, :] = jnp.where(lane >= DIFF_HEAD_DIM, q, zero)
    m_ref[...] = jnp.full(m_ref.shape, -jnp.inf, F32)
    acc_ref[...] = jnp.zeros(acc_ref.shape, F32)

    @pl.when(qi == 0)
    def _():
        vext_ref[:, 0:DIFF_V_DIM] = v_ref[...]
        vext_ref[:, DIFF_V_DIM:] = jnp.ones(v_ref.shape, BF16)

    def run_items(items):
        def start(j):
            return pl.multiple_of(j * tk, tk)

        def scores(item):
            j, _, row0 = item
            return _dot_nt(qs_ref[row0:row0 + tk, :], k_ref[pl.ds(start(j), tk), :])

        def softmax(item, s):
            _, triangular, row0 = item
            rows = slice(row0, row0 + tk)
            if triangular:
                row = lax.broadcasted_iota(jnp.int32, s.shape, 0)
                col = lax.broadcasted_iota(jnp.int32, s.shape, 1)
                s = jnp.where(col <= row, s, -jnp.inf)
            m_prev = m_ref[rows, :]
            m_new = jnp.maximum(m_prev, jnp.max(s, axis=-1, keepdims=True))
            alpha = jnp.exp2(m_prev - m_new)
            p = jnp.exp2((s - jnp.tile(m_new, (1, tk // LANES))).astype(BF16))
            m_ref[rows, :] = m_new
            return alpha, p

        def values(item, alpha, p):
            j, _, row0 = item
            rows = slice(row0, row0 + tk)
            acc_ref[rows, :] = (jnp.tile(alpha, (1, 2)) * acc_ref[rows, :]
                                + _dot(p, vext_ref[pl.ds(start(j), tk), :]))

        s_vals = {i: scores(items[i]) for i in range(min(ATTN_LOOKAHEAD, len(items)))}
        pending = None
        for i, item in enumerate(items):
            alpha_p = softmax(item, s_vals.pop(i))
            if i + ATTN_LOOKAHEAD < len(items):
                s_vals[i + ATTN_LOOKAHEAD] = scores(items[i + ATTN_LOOKAHEAD])
            if pending is not None:
                values(items[i - 1], *pending)
            pending = alpha_p
        values(items[-1], *pending)

    def row_start(comp, h):
        return comp * tq + h * tk

    def earlier_tile(i, carry):
        run_items([(halves * i + jj, False, row_start(comp, h))
                   for jj in range(halves) for comp in range(2) for h in range(halves)])
        return carry

    lax.fori_loop(0, qi, earlier_tile, 0)

    run_items([(halves * qi + jj, jj == h, row_start(comp, h))
               for jj in range(halves) for comp in range(2) for h in range(jj, halves)])

    lam_rows = lam_ref[...]
    lam = (jnp.exp(jnp.sum(lam_rows[0:1] * lam_rows[1:2], axis=-1, keepdims=True))
           - jnp.exp(jnp.sum(lam_rows[2:3] * lam_rows[3:4], axis=-1, keepdims=True))
           + lambda_init)
    o = acc_ref[:, 0:DIFF_V_DIM] / acc_ref[:, DIFF_V_DIM:]
    o = o[:tq] - lam * o[tq:]
    o = _rms(o, g_ref[...], SUBLN_EPS) * (1.0 - lambda_init)
    o_ref[...] = o.astype(BF16)


def _diff_attn(q, k, v, lam_rows, subln_g, lambda_init, batch, seq):
    t = q.shape[0]
    tq = ATTN_TILE
    nq = seq // tq
    q_spec = pl.BlockSpec((tq, DIFF_V_DIM), lambda b, h, i: (b * nq + i, h))
    kv_spec = pl.BlockSpec((seq, DIFF_V_DIM), lambda b, h, i: (b, h))
    return pl.pallas_call(
        functools.partial(_diff_attn_kernel, lambda_init=lambda_init),
        grid=(batch, DIFF_HEADS, nq),
        in_specs=[_resident((8, LANES)), q_spec, kv_spec, kv_spec, _resident((1, DIFF_V_DIM))],
        out_specs=q_spec,
        out_shape=jax.ShapeDtypeStruct((t, D_MODEL), BF16),
        scratch_shapes=[pltpu.VMEM((2 * tq, DIFF_V_DIM), BF16),
                        pltpu.VMEM((seq, 2 * DIFF_V_DIM), BF16),
                        pltpu.VMEM((2 * tq, LANES), F32),
                        pltpu.VMEM((2 * tq, 2 * DIFF_V_DIM), F32)],
        compiler_params=_params(3),
        name="diff_attn",
    )(lam_rows, q, k, v, subln_g)


def _wo_kernel(o_ref, w_ref, x_ref, y_ref):
    y_ref[...] = x_ref[...] + _dot(o_ref[...], w_ref[...])


def _out_proj(o, w, x, layer):
    t = x.shape[0]
    return pl.pallas_call(
        _wo_kernel,
        grid=(t // ROW_TILE,),
        in_specs=[_rows(ROW_TILE, D_MODEL), _layer_block(layer, (D_MODEL, D_MODEL)),
                  _rows(ROW_TILE, D_MODEL)],
        out_specs=_rows(ROW_TILE, D_MODEL),
        out_shape=jax.ShapeDtypeStruct((t, D_MODEL), F32),
        compiler_params=_params(1),
        name="diff_out_proj",
    )(o, w, x)


def kernel(x, mem, positions, norm_mix, norm_mem, norm_ffn, norm_final, conv_w_pw1, conv_b_pw1, conv_w_dw, conv_b_dw, conv_ln_g, conv_ln_b, conv_w_pw2, conv_b_pw2, kv_norm, w_k_shared, w_v_shared, diff_w_q, diff_lambda_q1, diff_lambda_k1, diff_lambda_q2, diff_lambda_k2, diff_subln_g, diff_w_o, mem_w_q, mem_w_k, mem_w_v, mem_w_o, ffn_w_gate, ffn_w_up, ffn_w_down):
    batch, seq, d = x.shape
    t = batch * seq
    row = lambda v: v.reshape(1, -1).astype(F32)
    bf = lambda w: w.astype(BF16)

    conv_w_pw1, conv_w_pw2, diff_w_q, diff_w_o = map(bf, (conv_w_pw1, conv_w_pw2, diff_w_q, diff_w_o))
    mem_w_q, mem_w_k, mem_w_v, mem_w_o = map(bf, (mem_w_q, mem_w_k, mem_w_v, mem_w_o))
    ffn_w_gate, ffn_w_up, ffn_w_down = map(bf, (ffn_w_gate, ffn_w_up, ffn_w_down))

    xs = x.reshape(t, d)
    tabs = _rope_tables(positions)
    mem_k, mem_v = _memkv(mem.reshape(-1, d), mem_w_k, mem_w_v)
    mem_k = mem_k.reshape(DEPTH, batch, -1, d)
    mem_v = mem_v.reshape(DEPTH, batch, -1, d)

    k_sh = v_sh = None
    for i in range(DEPTH):
        if i < N_A_LAYERS:
            u = _pw1(xs, row(norm_mix[i]), conv_w_pw1, row(conv_b_pw1[i]), i)
            wdw = jnp.repeat(conv_w_dw[i].astype(BF16), BF16_ROWS, axis=0)
            xs = _conv(u, xs, wdw, row(conv_b_dw[i]), row(conv_ln_g[i]), row(conv_ln_b[i]),
                       conv_w_pw2, row(conv_b_pw2[i]), i, batch, seq)
        else:
            b = i - N_A_LAYERS
            if b == 0:
                k_sh, v_sh = _shared_kv(xs, row(kv_norm), bf(w_k_shared), bf(w_v_shared), tabs)
            lambda_init = 0.8 - 0.6 * math.exp(-0.3 * i)
            q = _q_proj(xs, row(norm_mix[i]), diff_w_q, tabs, b)
            lam_rows = jnp.stack([diff_lambda_q1[b], diff_lambda_k1[b],
                                  diff_lambda_q2[b], diff_lambda_k2[b]]).astype(F32)
            lam_rows = jnp.pad(lam_rows, ((0, 4), (0, LANES - DIFF_HEAD_DIM)))
            o = _diff_attn(q, k_sh, v_sh, lam_rows, row(diff_subln_g[b]), lambda_init, batch, seq)
            xs = _out_proj(o, diff_w_o, xs, b)
        xs = _memattn(xs, row(norm_mem[i]), mem_w_q, mem_k, mem_v, mem_w_o, i, batch, seq)
        xs = _ffn(xs, row(norm_ffn[i]), ffn_w_gate, ffn_w_up, ffn_w_down,
                  row(norm_final), final_norm=(i == DEPTH - 1), layer=i)
    return xs.reshape(batch, seq, d)
```

```python
import functools
import math

import jax
import jax.numpy as jnp
from jax import lax
from jax.experimental import pallas as pl
from jax.experimental.pallas import tpu as pltpu

D_MODEL = 1024
DEPTH = 4
N_A_LAYERS = DEPTH // 2
CONV_WIDTH = 31
DIFF_HEADS = 8
DIFF_HEAD_DIM = 64
DIFF_V_DIM = 2 * DIFF_HEAD_DIM
ROT_DIM = DIFF_HEAD_DIM // 4
ROPE_THETA = 500000.0
MEM_HEADS = 4
MEM_HEAD_DIM = D_MODEL // MEM_HEADS
RMS_EPS = 1e-6
LN_EPS = 1e-5
SUBLN_EPS = 1e-5
LOG2_E = math.log2(math.e)

LANES = 128
SUBLANES = 8
BF16_ROWS = 16
CONV_HALO = 32
ROW_TILE = 1024
SUB_TILE = 128
ATTN_TILE = 4096
ATTN_KEY_TILE = 512
ATTN_LOOKAHEAD = 2
FFN_CHUNK = 256
VMEM_LIMIT = 56 * 1024 * 1024

BF16 = jnp.bfloat16
F32 = jnp.float32


def _dot(a, b):
    return jnp.dot(a, b, preferred_element_type=F32)


def _dot_nt(a, b):
    return lax.dot_general(a, b, (((1,), (1,)), ((), ())), preferred_element_type=F32)


def _rms(x, g, eps):
    return x * lax.rsqrt(jnp.mean(x * x, axis=-1, keepdims=True) + eps) * g


def _staged_rows(n_rows, stages):
    n = n_rows // SUB_TILE
    depth = len(stages)
    live = {}
    for step in range(n + depth - 1):
        for s, stage in enumerate(stages):
            c = step - s
            if 0 <= c < n:
                rows = slice(c * SUB_TILE, (c + 1) * SUB_TILE)
                live[c] = stage(rows) if s == 0 else stage(rows, live[c])


def _params(n_grid_dims):
    return pltpu.CompilerParams(
        dimension_semantics=("arbitrary",) * n_grid_dims, vmem_limit_bytes=VMEM_LIMIT)


def _resident(shape):
    zeros = (0,) * len(shape)
    return pl.BlockSpec(shape, lambda *_: zeros, pipeline_mode=pl.Buffered(1))


def _layer_block(layer, shape):
    zeros = (0,) * len(shape)
    return pl.BlockSpec((None,) + tuple(shape), lambda *_: (layer,) + zeros,
                        pipeline_mode=pl.Buffered(1))


def _rows(tile, width):
    return pl.BlockSpec((tile, width), lambda i: (i, 0))


def _rope_table_kernel(pos_ref, invf_ref, a_ref, b_ref, c_ref):
    ang = pos_ref[...].astype(F32) * invf_ref[...]
    cos = jnp.cos(ang)
    sin = jnp.sin(ang)
    lane = lax.broadcasted_iota(jnp.int32, ang.shape, 1) % DIFF_HEAD_DIM
    half = ROT_DIM // 2
    a_ref[...] = jnp.where(lane < ROT_DIM, cos, 1.0)
    b_ref[...] = jnp.where(lane < half, -sin, 0.0)
    c_ref[...] = jnp.where((lane >= half) & (lane < ROT_DIM), sin, 0.0)


def _rope_tables(positions):
    t = positions.size
    half = ROT_DIM // 2
    inv_freq = ROPE_THETA ** (-jnp.arange(0, ROT_DIM, 2, dtype=F32) / ROT_DIM)
    lane = jnp.arange(LANES) % DIFF_HEAD_DIM
    invf = jnp.where(lane < ROT_DIM, inv_freq[lane % half], 0.0).reshape(1, LANES).astype(F32)
    tile = 2048
    out = jax.ShapeDtypeStruct((t, LANES), F32)
    return pl.pallas_call(
        _rope_table_kernel,
        grid=(t // tile,),
        in_specs=[pl.BlockSpec((tile, 1), lambda i: (i, 0)), _resident((1, LANES))],
        out_specs=[_rows(tile, LANES)] * 3,
        out_shape=[out] * 3,
        compiler_params=_params(1),
        name="rope_tables",
    )(positions.reshape(t, 1), invf)


def _apply_rope(z, a, b, c):
    half = ROT_DIM // 2
    cols = []
    for j in range(z.shape[1] // LANES):
        zj = z[:, j * LANES:(j + 1) * LANES]
        cols.append(zj * a + pltpu.roll(zj, LANES - half, 1) * b + pltpu.roll(zj, half, 1) * c)
    return jnp.concatenate(cols, axis=1)


def _pw1_kernel(x_ref, g_ref, w_ref, b_ref, u_ref):
    h = _rms(x_ref[...], g_ref[...], RMS_EPS).astype(BF16)
    z = _dot(h, w_ref[...]) + b_ref[...]
    u_ref[...] = z[:, :D_MODEL] * jax.nn.sigmoid(z[:, D_MODEL:])


def _pw1(x, g, w, b, layer):
    t = x.shape[0]
    return pl.pallas_call(
        _pw1_kernel,
        grid=(t // ROW_TILE,),
        in_specs=[_rows(ROW_TILE, D_MODEL), _resident((1, D_MODEL)),
                  _layer_block(layer, (D_MODEL, 2 * D_MODEL)), _resident((1, 2 * D_MODEL))],
        out_specs=_rows(ROW_TILE, D_MODEL),
        out_shape=jax.ShapeDtypeStruct((t, D_MODEL), F32),
        compiler_params=_params(1),
        name="conv_pw1_glu",
    )(x, g, w, b)


CONV_ROW_CHUNK = 64


def _conv_kernel(u_ref, halo_ref, x_ref, wdw_ref, bdw_ref, lng_ref, lnb_ref, w2_ref, b2_ref,
                 o_ref, ext_ref, y_ref, win_ref):
    tile = u_ref.shape[0]
    first = pl.program_id(1) == 0
    ext_ref[0:CONV_HALO, :] = jnp.where(first, 0.0, halo_ref[...])
    ext_ref[CONV_HALO:, :] = u_ref[...]
    lead = CONV_HALO - (CONV_WIDTH - 1)

    def row_chunk(r, carry):
        r0 = pl.multiple_of(r * CONV_ROW_CHUNK, CONV_ROW_CHUNK)
        for c in range(D_MODEL // LANES):
            cs = slice(c * LANES, (c + 1) * LANES)
            win_ref[c] = ext_ref[pl.ds(r0, CONV_ROW_CHUNK + CONV_HALO), cs]
        for c in range(D_MODEL // LANES):
            cs = slice(c * LANES, (c + 1) * LANES)
            acc = jnp.zeros((CONV_ROW_CHUNK, LANES), F32)
            for j in range(CONV_WIDTH):
                w16 = wdw_ref[j * BF16_ROWS:(j + 1) * BF16_ROWS, cs]
                w_rows = jnp.concatenate([w16] * (CONV_ROW_CHUNK // BF16_ROWS), axis=0)
                u_rows = win_ref[c, lead + j:lead + j + CONV_ROW_CHUNK, :]
                acc = acc + u_rows.astype(BF16).astype(F32) * w_rows.astype(F32)
            y_ref[pl.ds(r0, CONV_ROW_CHUNK), cs] = acc
        return carry

    lax.fori_loop(0, tile // CONV_ROW_CHUNK, row_chunk, 0)

    y = y_ref[...] + bdw_ref[...]
    mu = jnp.mean(y, axis=-1, keepdims=True)
    yc = y - mu
    yn = yc * lax.rsqrt(jnp.mean(yc * yc, axis=-1, keepdims=True) + LN_EPS)
    yn = yn * lng_ref[...] + lnb_ref[...]
    act = (yn * jax.nn.sigmoid(yn)).astype(BF16)
    o_ref[...] = x_ref[...] + _dot(act, w2_ref[...]) + b2_ref[...]


def _conv(u, x, wdw, bdw, lng, lnb, w2, b2, layer, batch, seq):
    t = u.shape[0]
    tiles = seq // ROW_TILE
    halo_per_tile = ROW_TILE // CONV_HALO
    row_spec = pl.BlockSpec((ROW_TILE, D_MODEL), lambda b, i: (b * tiles + i, 0))
    halo_spec = pl.BlockSpec(
        (CONV_HALO, D_MODEL),
        lambda b, i: (jnp.maximum((b * tiles + i) * halo_per_tile - 1, 0), 0))
    return pl.pallas_call(
        _conv_kernel,
        grid=(batch, tiles),
        in_specs=[row_spec, halo_spec, row_spec,
                  _resident((CONV_WIDTH * BF16_ROWS, D_MODEL)), _resident((1, D_MODEL)),
                  _resident((1, D_MODEL)),
                  _resident((1, D_MODEL)), _layer_block(layer, (D_MODEL, D_MODEL)),
                  _resident((1, D_MODEL))],
        out_specs=row_spec,
        out_shape=jax.ShapeDtypeStruct((t, D_MODEL), F32),
        scratch_shapes=[pltpu.VMEM((ROW_TILE + CONV_HALO, D_MODEL), F32),
                        pltpu.VMEM((ROW_TILE, D_MODEL), F32),
                        pltpu.VMEM((D_MODEL // LANES, CONV_ROW_CHUNK + CONV_HALO, LANES), F32)],
        compiler_params=_params(2),
        name="conv_dw_ln_pw2",
    )(u, u, x, wdw, bdw, lng, lnb, w2, b2)


def _memkv_kernel(mem_ref, wk_ref, wv_ref, k_ref, v_ref):
    m = mem_ref[...].astype(BF16)
    k_ref[0] = _dot(m, wk_ref[0]).astype(BF16)
    v_ref[0] = _dot(m, wv_ref[0]).astype(BF16)


def _memkv(mem2d, wk, wv):
    rows = mem2d.shape[0]
    wspec = pl.BlockSpec((1, D_MODEL, D_MODEL), lambda i: (i, 0, 0))
    ospec = pl.BlockSpec((1, rows, D_MODEL), lambda i: (i, 0, 0))
    out = jax.ShapeDtypeStruct((DEPTH, rows, D_MODEL), BF16)
    return pl.pallas_call(
        _memkv_kernel,
        grid=(DEPTH,),
        in_specs=[_resident((rows, D_MODEL)), wspec, wspec],
        out_specs=[ospec, ospec],
        out_shape=[out, out],
        compiler_params=_params(1),
        name="mem_kv_proj",
    )(mem2d, wk, wv)


def _memattn_kernel(x_ref, g_ref, wq_ref, k_ref, v_ref, wo_ref, o_ref):
    x = x_ref[...]
    h = _rms(x, g_ref[...], RMS_EPS).astype(BF16)
    q = (_dot(h, wq_ref[...]) * (MEM_HEAD_DIM ** -0.5)).astype(BF16)
    def cols(hd):
        return slice(hd * MEM_HEAD_DIM, (hd + 1) * MEM_HEAD_DIM)

    def scores(hd):
        return _dot_nt(q[:, cols(hd)], k_ref[0, 0, :, cols(hd)])

    def softmax(s):
        e = jnp.exp(s - jnp.max(s, axis=-1, keepdims=True))
        return (e / jnp.sum(e, axis=-1, keepdims=True)).astype(BF16)

    heads = []
    s_next = scores(0)
    for hd in range(MEM_HEADS):
        s_cur = s_next
        if hd + 1 < MEM_HEADS:
            s_next = scores(hd + 1)
        p = softmax(s_cur)
        heads.append(_dot(p, v_ref[0, 0, :, cols(hd)]).astype(BF16))
    o = jnp.concatenate(heads, axis=-1)
    o_ref[...] = x + _dot(o, wo_ref[...])


def _memattn(x, g, wq, k_all, v_all, wo, layer, batch, seq):
    t = x.shape[0]
    tiles = seq // ROW_TILE
    m = k_all.shape[2]
    row_spec = pl.BlockSpec((ROW_TILE, D_MODEL), lambda b, i: (b * tiles + i, 0))
    kv_spec = pl.BlockSpec((1, 1, m, D_MODEL), lambda b, i: (layer, b, 0, 0))
    return pl.pallas_call(
        _memattn_kernel,
        grid=(batch, tiles),
        in_specs=[row_spec, _resident((1, D_MODEL)), _layer_block(layer, (D_MODEL, D_MODEL)),
                  kv_spec, kv_spec, _layer_block(layer, (D_MODEL, D_MODEL))],
        out_specs=row_spec,
        out_shape=jax.ShapeDtypeStruct((t, D_MODEL), F32),
        compiler_params=_params(2),
        name="mem_cross_attn",
    )(x, g, wq, k_all, v_all, wo)


def _ffn_kernel(x_ref, g_ref, wg_ref, wu_ref, wd_ref, gf_ref, o_ref, act_ref, *, final_norm):
    x = x_ref[...]
    h = _rms(x, g_ref[...], RMS_EPS).astype(BF16)
    hidden = wg_ref.shape[1]
    for c in range(hidden // FFN_CHUNK):
        cs = slice(c * FFN_CHUNK, (c + 1) * FFN_CHUNK)
        gate = _dot(h, wg_ref[:, cs])
        up = _dot(h, wu_ref[:, cs])
        act_ref[:, cs] = (gate * jax.nn.sigmoid(gate) * up).astype(BF16)
    y = x + _dot(act_ref[...], wd_ref[...])
    if final_norm:
        y = _rms(y, gf_ref[...], RMS_EPS)
    o_ref[...] = y


def _ffn(x, g, wg, wu, wd, g_final, final_norm, layer):
    t = x.shape[0]
    hidden = wg.shape[2]
    return pl.pallas_call(
        functools.partial(_ffn_kernel, final_norm=final_norm),
        grid=(t // ROW_TILE,),
        in_specs=[_rows(ROW_TILE, D_MODEL), _resident((1, D_MODEL)),
                  _layer_block(layer, (D_MODEL, hidden)), _layer_block(layer, (D_MODEL, hidden)),
                  _layer_block(layer, (hidden, D_MODEL)), _resident((1, D_MODEL))],
        out_specs=_rows(ROW_TILE, D_MODEL),
        out_shape=jax.ShapeDtypeStruct((t, D_MODEL), F32),
        scratch_shapes=[pltpu.VMEM((ROW_TILE, hidden), BF16)],
        compiler_params=_params(1),
        name="swiglu_ffn",
    )(x, g, wg, wu, wd, g_final)


def _kv_kernel(x_ref, g_ref, wk_ref, wv_ref, a_ref, b_ref, c_ref, k_ref, v_ref):
    h = _rms(x_ref[...], g_ref[...], RMS_EPS).astype(BF16)
    k = _apply_rope(_dot(h, wk_ref[...]), a_ref[...], b_ref[...], c_ref[...])
    k_ref[...] = k.astype(BF16)
    v_ref[...] = _dot(h, wv_ref[...]).astype(BF16)


def _shared_kv(x, g, wk, wv, tabs):
    t = x.shape[0]
    out = jax.ShapeDtypeStruct((t, D_MODEL), BF16)
    return pl.pallas_call(
        _kv_kernel,
        grid=(t // ROW_TILE,),
        in_specs=[_rows(ROW_TILE, D_MODEL), _resident((1, D_MODEL)),
                  _resident((D_MODEL, D_MODEL)), _resident((D_MODEL, D_MODEL))]
                 + [_rows(ROW_TILE, LANES)] * 3,
        out_specs=[_rows(ROW_TILE, D_MODEL)] * 2,
        out_shape=[out, out],
        compiler_params=_params(1),
        name="shared_kv_proj",
    )(x, g, wk, wv, *tabs)


def _q_kernel(x_ref, g_ref, wq_ref, a_ref, b_ref, c_ref, q_ref):
    def norm(rows):
        return _rms(x_ref[rows, :], g_ref[...], RMS_EPS).astype(BF16)

    def project(rows, h):
        return _dot(h, wq_ref[...])

    def finish(rows, z):
        q = _apply_rope(z, a_ref[rows, :], b_ref[rows, :], c_ref[rows, :])
        q_ref[rows, :] = (q * (DIFF_HEAD_DIM ** -0.5 * LOG2_E)).astype(BF16)

    _staged_rows(x_ref.shape[0], [norm, project, finish])


def _q_proj(x, g, wq, tabs, layer):
    t = x.shape[0]
    return pl.pallas_call(
        _q_kernel,
        grid=(t // ROW_TILE,),
        in_specs=[_rows(ROW_TILE, D_MODEL), _resident((1, D_MODEL)),
                  _layer_block(layer, (D_MODEL, D_MODEL))] + [_rows(ROW_TILE, LANES)] * 3,
        out_specs=_rows(ROW_TILE, D_MODEL),
        out_shape=jax.ShapeDtypeStruct((t, D_MODEL), BF16),
        compiler_params=_params(1),
        name="diff_q_proj",
    )(x, g, wq, *tabs)


def _diff_attn_kernel(lam_ref, q_ref, k_ref, v_ref, g_ref, o_ref,
                      qs_ref, vext_ref, m_ref, acc_ref, *, lambda_init, single_tile):
    tq = q_ref.shape[0]
    tk = ATTN_KEY_TILE
    halves = tq // tk
    qi = pl.program_id(2)
    q = q_ref[...]
    lane = lax.broadcasted_iota(jnp.int32, q.shape, 1)
    zero = jnp.zeros_like(q)
    qs_ref[0:tq, :] = jnp.where(lane < DIFF_HEAD_DIM, q, zero)
    qs_ref[tq:, :] = jnp.where(lane >= DIFF_HEAD_DIM, q, zero)
    m_ref[...] = jnp.full(m_ref.shape, -jnp.inf, F32)
    acc_ref[...] = jnp.zeros(acc_ref.shape, F32)

    @pl.when(qi == 0)
    def _():
        vext_ref[:, 0:DIFF_V_DIM] = v_ref[...]
        vext_ref[:, DIFF_V_DIM:] = jnp.ones(v_ref.shape, BF16)

    def run_items(items):
        def start(j):
            return pl.multiple_of(j * tk, tk)

        def scores(item):
            j, _, row0 = item
            return _dot_nt(qs_ref[row0:row0 + tk, :], k_ref[pl.ds(start(j), tk), :])

        def softmax(item, s):
            _, triangular, row0 = item
            rows = slice(row0, row0 + tk)
            if triangular:
                row = lax.broadcasted_iota(jnp.int32, s.shape, 0)
                col = lax.broadcasted_iota(jnp.int32, s.shape, 1)
                s = jnp.where(col <= row, s, -jnp.inf)
            m_prev = m_ref[rows, :]
            m_new = jnp.maximum(m_prev, jnp.max(s, axis=-1, keepdims=True))
            alpha = jnp.exp2(m_prev - m_new)
            p = jnp.exp2((s - jnp.tile(m_new, (1, tk // LANES))).astype(BF16))
            m_ref[rows, :] = m_new
            return alpha, p

        def values(item, alpha, p):
            j, _, row0 = item
            rows = slice(row0, row0 + tk)
            acc_ref[rows, :] = (jnp.tile(alpha, (1, 2)) * acc_ref[rows, :]
                                + _dot(p, vext_ref[pl.ds(start(j), tk), :]))

        s_vals = {i: scores(items[i]) for i in range(min(ATTN_LOOKAHEAD, len(items)))}
        pending = None
        for i, item in enumerate(items):
            alpha_p = softmax(item, s_vals.pop(i))
            if i + ATTN_LOOKAHEAD < len(items):
                s_vals[i + ATTN_LOOKAHEAD] = scores(items[i + ATTN_LOOKAHEAD])
            if pending is not None:
                values(items[i - 1], *pending)
            pending = alpha_p
        values(items[-1], *pending)

    def row_start(comp, h):
        return comp * tq + h * tk

    def earlier_tile(i, carry):
        run_items([(halves * i + jj, False, row_start(comp, h))
                   for jj in range(halves) for comp in range(2) for h in range(halves)])
        return carry

    if not single_tile:
        lax.fori_loop(0, qi, earlier_tile, 0)

    run_items([(halves * qi + jj, jj == h, row_start(comp, h))
               for jj in range(halves) for comp in range(2) for h in range(jj, halves)])

    lam_rows = lam_ref[...]
    lam = (jnp.exp(jnp.sum(lam_rows[0:1] * lam_rows[1:2], axis=-1, keepdims=True))
           - jnp.exp(jnp.sum(lam_rows[2:3] * lam_rows[3:4], axis=-1, keepdims=True))
           + lambda_init)
    o = acc_ref[:, 0:DIFF_V_DIM] / acc_ref[:, DIFF_V_DIM:]
    o = o[:tq] - lam * o[tq:]
    o = _rms(o, g_ref[...], SUBLN_EPS) * (1.0 - lambda_init)
    o_ref[...] = o.astype(BF16)


def _diff_attn(q, k, v, lam_rows, subln_g, lambda_init, batch, seq):
    t = q.shape[0]
    tq = ATTN_TILE
    nq = seq // tq
    q_spec = pl.BlockSpec((tq, DIFF_V_DIM), lambda b, h, i: (b * nq + i, h))
    kv_spec = pl.BlockSpec((seq, DIFF_V_DIM), lambda b, h, i: (b, h))
    return pl.pallas_call(
        functools.partial(_diff_attn_kernel, lambda_init=lambda_init, single_tile=(nq == 1)),
        grid=(batch, DIFF_HEADS, nq),
        in_specs=[_resident((8, LANES)), q_spec, kv_spec, kv_spec, _resident((1, DIFF_V_DIM))],
        out_specs=q_spec,
        out_shape=jax.ShapeDtypeStruct((t, D_MODEL), BF16),
        scratch_shapes=[pltpu.VMEM((2 * tq, DIFF_V_DIM), BF16),
                        pltpu.VMEM((seq, 2 * DIFF_V_DIM), BF16),
                        pltpu.VMEM((2 * tq, LANES), F32),
                        pltpu.VMEM((2 * tq, 2 * DIFF_V_DIM), F32)],
        compiler_params=_params(3),
        name="diff_attn",
    )(lam_rows, q, k, v, subln_g)


def _wo_kernel(o_ref, w_ref, x_ref, y_ref):
    y_ref[...] = x_ref[...] + _dot(o_ref[...], w_ref[...])


def _out_proj(o, w, x, layer):
    t = x.shape[0]
    return pl.pallas_call(
        _wo_kernel,
        grid=(t // ROW_TILE,),
        in_specs=[_rows(ROW_TILE, D_MODEL), _layer_block(layer, (D_MODEL, D_MODEL)),
                  _rows(ROW_TILE, D_MODEL)],
        out_specs=_rows(ROW_TILE, D_MODEL),
        out_shape=jax.ShapeDtypeStruct((t, D_MODEL), F32),
        compiler_params=_params(1),
        name="diff_out_proj",
    )(o, w, x)


def kernel(x, mem, positions, norm_mix, norm_mem, norm_ffn, norm_final, conv_w_pw1, conv_b_pw1, conv_w_dw, conv_b_dw, conv_ln_g, conv_ln_b, conv_w_pw2, conv_b_pw2, kv_norm, w_k_shared, w_v_shared, diff_w_q, diff_lambda_q1, diff_lambda_k1, diff_lambda_q2, diff_lambda_k2, diff_subln_g, diff_w_o, mem_w_q, mem_w_k, mem_w_v, mem_w_o, ffn_w_gate, ffn_w_up, ffn_w_down):
    batch, seq, d = x.shape
    t = batch * seq
    row = lambda v: v.reshape(1, -1).astype(F32)
    bf = lambda w: w.astype(BF16)

    conv_w_pw1, conv_w_pw2, diff_w_q, diff_w_o = map(bf, (conv_w_pw1, conv_w_pw2, diff_w_q, diff_w_o))
    mem_w_q, mem_w_k, mem_w_v, mem_w_o = map(bf, (mem_w_q, mem_w_k, mem_w_v, mem_w_o))
    ffn_w_gate, ffn_w_up, ffn_w_down = map(bf, (ffn_w_gate, ffn_w_up, ffn_w_down))

    xs = x.reshape(t, d)
    tabs = _rope_tables(positions)
    mem_k, mem_v = _memkv(mem.reshape(-1, d), mem_w_k, mem_w_v)
    mem_k = mem_k.reshape(DEPTH, batch, -1, d)
    mem_v = mem_v.reshape(DEPTH, batch, -1, d)

    k_sh = v_sh = None
    for i in range(DEPTH):
        if i < N_A_LAYERS:
            u = _pw1(xs, row(norm_mix[i]), conv_w_pw1, row(conv_b_pw1[i]), i)
            wdw = jnp.repeat(conv_w_dw[i].astype(BF16), BF16_ROWS, axis=0)
            xs = _conv(u, xs, wdw, row(conv_b_dw[i]), row(conv_ln_g[i]), row(conv_ln_b[i]),
                       conv_w_pw2, row(conv_b_pw2[i]), i, batch, seq)
        else:
            b = i - N_A_LAYERS
            if b == 0:
                k_sh, v_sh = _shared_kv(xs, row(kv_norm), bf(w_k_shared), bf(w_v_shared), tabs)
            lambda_init = 0.8 - 0.6 * math.exp(-0.3 * i)
            q = _q_proj(xs, row(norm_mix[i]), diff_w_q, tabs, b)
            lam_rows = jnp.stack([diff_lambda_q1[b], diff_lambda_k1[b],
                                  diff_lambda_q2[b], diff_lambda_k2[b]]).astype(F32)
            lam_rows = jnp.pad(lam_rows, ((0, 4), (0, LANES - DIFF_HEAD_DIM)))
            o = _diff_attn(q, k_sh, v_sh, lam_rows, row(diff_subln_g[b]), lambda_init, batch, seq)
            xs = _out_proj(o, diff_w_o, xs, b)
        xs = _memattn(xs, row(norm_mem[i]), mem_w_q, mem_k, mem_v, mem_w_o, i, batch, seq)
        xs = _ffn(xs, row(norm_ffn[i]), ffn_w_gate, ffn_w_up, ffn_w_down,
                  row(norm_final), final_norm=(i == DEPTH - 1), layer=i)
    return xs.reshape(batch, seq, d)
```

```python
import functools
import math

import jax
import jax.numpy as jnp
from jax import lax
from jax.experimental import pallas as pl
from jax.experimental.pallas import tpu as pltpu

D_MODEL = 1024
DEPTH = 4
N_A_LAYERS = DEPTH // 2
CONV_WIDTH = 31
DIFF_HEADS = 8
DIFF_HEAD_DIM = 64
DIFF_V_DIM = 2 * DIFF_HEAD_DIM
ROT_DIM = DIFF_HEAD_DIM // 4
ROPE_THETA = 500000.0
MEM_HEADS = 4
MEM_HEAD_DIM = D_MODEL // MEM_HEADS
RMS_EPS = 1e-6
LN_EPS = 1e-5
SUBLN_EPS = 1e-5
LOG2_E = math.log2(math.e)

LANES = 128
SUBLANES = 8
BF16_ROWS = 16
CONV_HALO = 32
ROW_TILE = 1024
SUB_TILE = 128
ATTN_TILE = 4096
ATTN_KEY_TILE = 512
ATTN_LOOKAHEAD = 2
FFN_CHUNK = 256
VMEM_LIMIT = 56 * 1024 * 1024

BF16 = jnp.bfloat16
F32 = jnp.float32


def _dot(a, b):
    return jnp.dot(a, b, preferred_element_type=F32)


def _dot_nt(a, b):
    return lax.dot_general(a, b, (((1,), (1,)), ((), ())), preferred_element_type=F32)


def _rms(x, g, eps):
    return x * lax.rsqrt(jnp.mean(x * x, axis=-1, keepdims=True) + eps) * g


def _staged_rows(n_rows, stages):
    n = n_rows // SUB_TILE
    depth = len(stages)
    live = {}
    for step in range(n + depth - 1):
        for s, stage in enumerate(stages):
            c = step - s
            if 0 <= c < n:
                rows = slice(c * SUB_TILE, (c + 1) * SUB_TILE)
                live[c] = stage(rows) if s == 0 else stage(rows, live[c])


def _params(n_grid_dims):
    return pltpu.CompilerParams(
        dimension_semantics=("arbitrary",) * n_grid_dims, vmem_limit_bytes=VMEM_LIMIT)


def _resident(shape):
    zeros = (0,) * len(shape)
    return pl.BlockSpec(shape, lambda *_: zeros, pipeline_mode=pl.Buffered(1))


def _layer_block(layer, shape):
    zeros = (0,) * len(shape)
    return pl.BlockSpec((None,) + tuple(shape), lambda *_: (layer,) + zeros,
                        pipeline_mode=pl.Buffered(1))


def _rows(tile, width):
    return pl.BlockSpec((tile, width), lambda i: (i, 0))


def _rope_table_kernel(pos_ref, invf_ref, a_ref, b_ref, c_ref):
    ang = pos_ref[...].astype(F32) * invf_ref[...]
    cos = jnp.cos(ang)
    sin = jnp.sin(ang)
    lane = lax.broadcasted_iota(jnp.int32, ang.shape, 1) % DIFF_HEAD_DIM
    half = ROT_DIM // 2
    a_ref[...] = jnp.where(lane < ROT_DIM, cos, 1.0)
    b_ref[...] = jnp.where(lane < half, -sin, 0.0)
    c_ref[...] = jnp.where((lane >= half) & (lane < ROT_DIM), sin, 0.0)


def _rope_tables(positions):
    t = positions.size
    half = ROT_DIM // 2
    inv_freq = ROPE_THETA ** (-jnp.arange(0, ROT_DIM, 2, dtype=F32) / ROT_DIM)
    lane = jnp.arange(LANES) % DIFF_HEAD_DIM
    invf = jnp.where(lane < ROT_DIM, inv_freq[lane % half], 0.0).reshape(1, LANES).astype(F32)
    tile = 2048
    out = jax.ShapeDtypeStruct((t, LANES), F32)
    return pl.pallas_call(
        _rope_table_kernel,
        grid=(t // tile,),
        in_specs=[pl.BlockSpec((tile, 1), lambda i: (i, 0)), _resident((1, LANES))],
        out_specs=[_rows(tile, LANES)] * 3,
        out_shape=[out] * 3,
        compiler_params=_params(1),
        name="rope_tables",
    )(positions.reshape(t, 1), invf)


def _apply_rope(z, a, b, c):
    half = ROT_DIM // 2
    cols = []
    for j in range(z.shape[1] // LANES):
        zj = z[:, j * LANES:(j + 1) * LANES]
        cols.append(zj * a + pltpu.roll(zj, LANES - half, 1) * b + pltpu.roll(zj, half, 1) * c)
    return jnp.concatenate(cols, axis=1)


def _pw1_kernel(x_ref, g_ref, w_ref, b_ref, u_ref):
    h = _rms(x_ref[...], g_ref[...], RMS_EPS).astype(BF16)
    z = _dot(h, w_ref[...]) + b_ref[...]
    u_ref[...] = z[:, :D_MODEL] * jax.nn.sigmoid(z[:, D_MODEL:])


def _pw1(x, g, w, b, layer):
    t = x.shape[0]
    return pl.pallas_call(
        _pw1_kernel,
        grid=(t // ROW_TILE,),
        in_specs=[_rows(ROW_TILE, D_MODEL), _resident((1, D_MODEL)),
                  _layer_block(layer, (D_MODEL, 2 * D_MODEL)), _resident((1, 2 * D_MODEL))],
        out_specs=_rows(ROW_TILE, D_MODEL),
        out_shape=jax.ShapeDtypeStruct((t, D_MODEL), F32),
        compiler_params=_params(1),
        name="conv_pw1_glu",
    )(x, g, w, b)


CONV_ROW_CHUNK = 64


def _conv_kernel(u_ref, halo_ref, x_ref, wdw_ref, bdw_ref, lng_ref, lnb_ref, w2_ref, b2_ref,
                 o_ref, ext_ref, y_ref, win_ref):
    tile = u_ref.shape[0]
    first = pl.program_id(1) == 0
    ext_ref[0:CONV_HALO, :] = jnp.where(first, 0.0, halo_ref[...])
    ext_ref[CONV_HALO:, :] = u_ref[...]
    lead = CONV_HALO - (CONV_WIDTH - 1)

    def row_chunk(r, carry):
        r0 = pl.multiple_of(r * CONV_ROW_CHUNK, CONV_ROW_CHUNK)
        for c in range(D_MODEL // LANES):
            cs = slice(c * LANES, (c + 1) * LANES)
            win_ref[c] = ext_ref[pl.ds(r0, CONV_ROW_CHUNK + CONV_HALO), cs]
        for c in range(D_MODEL // LANES):
            cs = slice(c * LANES, (c + 1) * LANES)
            acc = jnp.zeros((CONV_ROW_CHUNK, LANES), F32)
            for j in range(CONV_WIDTH):
                w16 = wdw_ref[j * BF16_ROWS:(j + 1) * BF16_ROWS, cs]
                w_rows = jnp.concatenate([w16] * (CONV_ROW_CHUNK // BF16_ROWS), axis=0)
                u_rows = win_ref[c, lead + j:lead + j + CONV_ROW_CHUNK, :]
                acc = acc + u_rows.astype(BF16).astype(F32) * w_rows.astype(F32)
            y_ref[pl.ds(r0, CONV_ROW_CHUNK), cs] = acc
        return carry

    lax.fori_loop(0, tile // CONV_ROW_CHUNK, row_chunk, 0)

    y = y_ref[...] + bdw_ref[...]
    mu = jnp.mean(y, axis=-1, keepdims=True)
    yc = y - mu
    yn = yc * lax.rsqrt(jnp.mean(yc * yc, axis=-1, keepdims=True) + LN_EPS)
    yn = yn * lng_ref[...] + lnb_ref[...]
    act = (yn * jax.nn.sigmoid(yn)).astype(BF16)
    o_ref[...] = x_ref[...] + _dot(act, w2_ref[...]) + b2_ref[...]


def _conv(u, x, wdw, bdw, lng, lnb, w2, b2, layer, batch, seq):
    t = u.shape[0]
    tiles = seq // ROW_TILE
    halo_per_tile = ROW_TILE // CONV_HALO
    row_spec = pl.BlockSpec((ROW_TILE, D_MODEL), lambda b, i: (b * tiles + i, 0))
    halo_spec = pl.BlockSpec(
        (CONV_HALO, D_MODEL),
        lambda b, i: (jnp.maximum((b * tiles + i) * halo_per_tile - 1, 0), 0))
    return pl.pallas_call(
        _conv_kernel,
        grid=(batch, tiles),
        in_specs=[row_spec, halo_spec, row_spec,
                  _resident((CONV_WIDTH * BF16_ROWS, D_MODEL)), _resident((1, D_MODEL)),
                  _resident((1, D_MODEL)),
                  _resident((1, D_MODEL)), _layer_block(layer, (D_MODEL, D_MODEL)),
                  _resident((1, D_MODEL))],
        out_specs=row_spec,
        out_shape=jax.ShapeDtypeStruct((t, D_MODEL), F32),
        scratch_shapes=[pltpu.VMEM((ROW_TILE + CONV_HALO, D_MODEL), F32),
                        pltpu.VMEM((ROW_TILE, D_MODEL), F32),
                        pltpu.VMEM((D_MODEL // LANES, CONV_ROW_CHUNK + CONV_HALO, LANES), F32)],
        compiler_params=_params(2),
        name="conv_dw_ln_pw2",
    )(u, u, x, wdw, bdw, lng, lnb, w2, b2)


def _memkv_kernel(mem_ref, wk_ref, wv_ref, k_ref, v_ref):
    m = mem_ref[...].astype(BF16)
    k_ref[0] = _dot(m, wk_ref[0]).astype(BF16)
    v_ref[0] = _dot(m, wv_ref[0]).astype(BF16)


def _memkv(mem2d, wk, wv):
    rows = mem2d.shape[0]
    wspec = pl.BlockSpec((1, D_MODEL, D_MODEL), lambda i: (i, 0, 0))
    ospec = pl.BlockSpec((1, rows, D_MODEL), lambda i: (i, 0, 0))
    out = jax.ShapeDtypeStruct((DEPTH, rows, D_MODEL), BF16)
    return pl.pallas_call(
        _memkv_kernel,
        grid=(DEPTH,),
        in_specs=[_resident((rows, D_MODEL)), wspec, wspec],
        out_specs=[ospec, ospec],
        out_shape=[out, out],
        compiler_params=_params(1),
        name="mem_kv_proj",
    )(mem2d, wk, wv)


def _memattn_kernel(*refs, with_attn_out):
    if with_attn_out:
        a_ref, wa_ref, *refs = refs
    x_ref, g_ref, wq_ref, k_ref, v_ref, wo_ref, o_ref = refs
    x = x_ref[...]
    if with_attn_out:
        x = x + _dot(a_ref[...], wa_ref[...])
    h = _rms(x, g_ref[...], RMS_EPS).astype(BF16)
    q = (_dot(h, wq_ref[...]) * (MEM_HEAD_DIM ** -0.5)).astype(BF16)

    def cols(hd):
        return slice(hd * MEM_HEAD_DIM, (hd + 1) * MEM_HEAD_DIM)

    def scores(hd):
        return _dot_nt(q[:, cols(hd)], k_ref[0, 0, :, cols(hd)])

    def softmax(s):
        e = jnp.exp(s - jnp.max(s, axis=-1, keepdims=True))
        return (e / jnp.sum(e, axis=-1, keepdims=True)).astype(BF16)

    heads = []
    s_next = scores(0)
    for hd in range(MEM_HEADS):
        s_cur = s_next
        if hd + 1 < MEM_HEADS:
            s_next = scores(hd + 1)
        p = softmax(s_cur)
        heads.append(_dot(p, v_ref[0, 0, :, cols(hd)]).astype(BF16))
    o = jnp.concatenate(heads, axis=-1)
    o_ref[...] = x + _dot(o, wo_ref[...])


def _memattn(x, g, wq, k_all, v_all, wo, layer, batch, seq, attn_out=None, w_attn_out=None,
             attn_layer=0):
    t = x.shape[0]
    tiles = seq // ROW_TILE
    m = k_all.shape[2]
    square = (D_MODEL, D_MODEL)
    row_spec = pl.BlockSpec((ROW_TILE, D_MODEL), lambda b, i: (b * tiles + i, 0))
    kv_spec = pl.BlockSpec((1, 1, m, D_MODEL), lambda b, i: (layer, b, 0, 0))
    in_specs = [row_spec, _resident((1, D_MODEL)), _layer_block(layer, square),
                kv_spec, kv_spec, _layer_block(layer, square)]
    args = [x, g, wq, k_all, v_all, wo]
    with_attn_out = attn_out is not None
    if with_attn_out:
        in_specs = [row_spec, _layer_block(attn_layer, square)] + in_specs
        args = [attn_out, w_attn_out] + args
    return pl.pallas_call(
        functools.partial(_memattn_kernel, with_attn_out=with_attn_out),
        grid=(batch, tiles),
        in_specs=in_specs,
        out_specs=row_spec,
        out_shape=jax.ShapeDtypeStruct((t, D_MODEL), F32),
        compiler_params=_params(2),
        name="attn_out_mem_cross_attn" if with_attn_out else "mem_cross_attn",
    )(*args)


def _ffn_kernel(x_ref, g_ref, wg_ref, wu_ref, wd_ref, gf_ref, o_ref, act_ref, *, final_norm):
    x = x_ref[...]
    h = _rms(x, g_ref[...], RMS_EPS).astype(BF16)
    hidden = wg_ref.shape[1]
    for c in range(hidden // FFN_CHUNK):
        cs = slice(c * FFN_CHUNK, (c + 1) * FFN_CHUNK)
        gate = _dot(h, wg_ref[:, cs])
        up = _dot(h, wu_ref[:, cs])
        act_ref[:, cs] = (gate * jax.nn.sigmoid(gate) * up).astype(BF16)
    y = x + _dot(act_ref[...], wd_ref[...])
    if final_norm:
        y = _rms(y, gf_ref[...], RMS_EPS)
    o_ref[...] = y


def _ffn(x, g, wg, wu, wd, g_final, final_norm, layer):
    t = x.shape[0]
    hidden = wg.shape[2]
    return pl.pallas_call(
        functools.partial(_ffn_kernel, final_norm=final_norm),
        grid=(t // ROW_TILE,),
        in_specs=[_rows(ROW_TILE, D_MODEL), _resident((1, D_MODEL)),
                  _layer_block(layer, (D_MODEL, hidden)), _layer_block(layer, (D_MODEL, hidden)),
                  _layer_block(layer, (hidden, D_MODEL)), _resident((1, D_MODEL))],
        out_specs=_rows(ROW_TILE, D_MODEL),
        out_shape=jax.ShapeDtypeStruct((t, D_MODEL), F32),
        scratch_shapes=[pltpu.VMEM((ROW_TILE, hidden), BF16)],
        compiler_params=_params(1),
        name="swiglu_ffn",
    )(x, g, wg, wu, wd, g_final)


def _kv_kernel(x_ref, g_ref, wk_ref, wv_ref, a_ref, b_ref, c_ref, k_ref, v_ref):
    h = _rms(x_ref[...], g_ref[...], RMS_EPS).astype(BF16)
    k = _apply_rope(_dot(h, wk_ref[...]), a_ref[...], b_ref[...], c_ref[...])
    k_ref[...] = k.astype(BF16)
    v_ref[...] = _dot(h, wv_ref[...]).astype(BF16)


def _shared_kv(x, g, wk, wv, tabs):
    t = x.shape[0]
    out = jax.ShapeDtypeStruct((t, D_MODEL), BF16)
    return pl.pallas_call(
        _kv_kernel,
        grid=(t // ROW_TILE,),
        in_specs=[_rows(ROW_TILE, D_MODEL), _resident((1, D_MODEL)),
                  _resident((D_MODEL, D_MODEL)), _resident((D_MODEL, D_MODEL))]
                 + [_rows(ROW_TILE, LANES)] * 3,
        out_specs=[_rows(ROW_TILE, D_MODEL)] * 2,
        out_shape=[out, out],
        compiler_params=_params(1),
        name="shared_kv_proj",
    )(x, g, wk, wv, *tabs)


def _q_kernel(x_ref, g_ref, wq_ref, a_ref, b_ref, c_ref, q_ref):
    def norm(rows):
        return _rms(x_ref[rows, :], g_ref[...], RMS_EPS).astype(BF16)

    def project(rows, h):
        return _dot(h, wq_ref[...])

    def finish(rows, z):
        q = _apply_rope(z, a_ref[rows, :], b_ref[rows, :], c_ref[rows, :])
        q_ref[rows, :] = (q * (DIFF_HEAD_DIM ** -0.5 * LOG2_E)).astype(BF16)

    _staged_rows(x_ref.shape[0], [norm, project, finish])


def _q_proj(x, g, wq, tabs, layer):
    t = x.shape[0]
    return pl.pallas_call(
        _q_kernel,
        grid=(t // ROW_TILE,),
        in_specs=[_rows(ROW_TILE, D_MODEL), _resident((1, D_MODEL)),
                  _layer_block(layer, (D_MODEL, D_MODEL))] + [_rows(ROW_TILE, LANES)] * 3,
        out_specs=_rows(ROW_TILE, D_MODEL),
        out_shape=jax.ShapeDtypeStruct((t, D_MODEL), BF16),
        compiler_params=_params(1),
        name="diff_q_proj",
    )(x, g, wq, *tabs)


def _diff_attn_kernel(lam_ref, q_ref, k_ref, v_ref, g_ref, o_ref,
                      qs_ref, vext_ref, m_ref, acc_ref, *, lambda_init, single_tile):
    tq = q_ref.shape[0]
    tk = ATTN_KEY_TILE
    halves = tq // tk
    qi = pl.program_id(2)
    q = q_ref[...]
    lane = lax.broadcasted_iota(jnp.int32, q.shape, 1)
    zero = jnp.zeros_like(q)
    qs_ref[0:tq, :] = jnp.where(lane < DIFF_HEAD_DIM, q, zero)
    qs_ref[tq:, :] = jnp.where(lane >= DIFF_HEAD_DIM, q, zero)
    m_ref[...] = jnp.full(m_ref.shape, -jnp.inf, F32)
    acc_ref[...] = jnp.zeros(acc_ref.shape, F32)

    @pl.when(qi == 0)
    def _():
        vext_ref[:, 0:DIFF_V_DIM] = v_ref[...]
        vext_ref[:, DIFF_V_DIM:] = jnp.ones(v_ref.shape, BF16)

    def run_items(items):
        def start(j):
            return pl.multiple_of(j * tk, tk)

        def scores(item):
            j, _, row0 = item
            return _dot_nt(qs_ref[row0:row0 + tk, :], k_ref[pl.ds(start(j), tk), :])

        def softmax(item, s):
            _, triangular, row0 = item
            rows = slice(row0, row0 + tk)
            if triangular:
                row = lax.broadcasted_iota(jnp.int32, s.shape, 0)
                col = lax.broadcasted_iota(jnp.int32, s.shape, 1)
                s = jnp.where(col <= row, s, -jnp.inf)
            m_prev = m_ref[rows, :]
            m_new = jnp.maximum(m_prev, jnp.max(s, axis=-1, keepdims=True))
            alpha = jnp.exp2(m_prev - m_new)
            p = jnp.exp2((s - jnp.tile(m_new, (1, tk // LANES))).astype(BF16))
            m_ref[rows, :] = m_new
            return alpha, p

        def values(item, alpha, p):
            j, _, row0 = item
            rows = slice(row0, row0 + tk)
            acc_ref[rows, :] = (jnp.tile(alpha, (1, 2)) * acc_ref[rows, :]
                                + _dot(p, vext_ref[pl.ds(start(j), tk), :]))

        s_vals = {i: scores(items[i]) for i in range(min(ATTN_LOOKAHEAD, len(items)))}
        pending = None
        for i, item in enumerate(items):
            alpha_p = softmax(item, s_vals.pop(i))
            if i + ATTN_LOOKAHEAD < len(items):
                s_vals[i + ATTN_LOOKAHEAD] = scores(items[i + ATTN_LOOKAHEAD])
            if pending is not None:
                values(items[i - 1], *pending)
            pending = alpha_p
        values(items[-1], *pending)

    def row_start(comp, h):
        return comp * tq + h * tk

    def earlier_tile(i, carry):
        run_items([(halves * i + jj, False, row_start(comp, h))
                   for jj in range(halves) for comp in range(2) for h in range(halves)])
        return carry

    if not single_tile:
        lax.fori_loop(0, qi, earlier_tile, 0)

    run_items([(halves * qi + jj, jj == h, row_start(comp, h))
               for jj in range(halves) for comp in range(2) for h in range(jj, halves)])

    lam_rows = lam_ref[...]
    lam = (jnp.exp(jnp.sum(lam_rows[0:1] * lam_rows[1:2], axis=-1, keepdims=True))
           - jnp.exp(jnp.sum(lam_rows[2:3] * lam_rows[3:4], axis=-1, keepdims=True))
           + lambda_init)
    o = acc_ref[:, 0:DIFF_V_DIM] / acc_ref[:, DIFF_V_DIM:]
    o = o[:tq] - lam * o[tq:]
    o = _rms(o, g_ref[...], SUBLN_EPS) * (1.0 - lambda_init)
    o_ref[...] = o.astype(BF16)


def _diff_attn(q, k, v, lam_rows, subln_g, lambda_init, batch, seq):
    t = q.shape[0]
    tq = ATTN_TILE
    nq = seq // tq
    q_spec = pl.BlockSpec((tq, DIFF_V_DIM), lambda b, h, i: (b * nq + i, h))
    kv_spec = pl.BlockSpec((seq, DIFF_V_DIM), lambda b, h, i: (b, h))
    return pl.pallas_call(
        functools.partial(_diff_attn_kernel, lambda_init=lambda_init, single_tile=(nq == 1)),
        grid=(batch, DIFF_HEADS, nq),
        in_specs=[_resident((8, LANES)), q_spec, kv_spec, kv_spec, _resident((1, DIFF_V_DIM))],
        out_specs=q_spec,
        out_shape=jax.ShapeDtypeStruct((t, D_MODEL), BF16),
        scratch_shapes=[pltpu.VMEM((2 * tq, DIFF_V_DIM), BF16),
                        pltpu.VMEM((seq, 2 * DIFF_V_DIM), BF16),
                        pltpu.VMEM((2 * tq, LANES), F32),
                        pltpu.VMEM((2 * tq, 2 * DIFF_V_DIM), F32)],
        compiler_params=_params(3),
        name="diff_attn",
    )(lam_rows, q, k, v, subln_g)


def kernel(x, mem, positions, norm_mix, norm_mem, norm_ffn, norm_final, conv_w_pw1, conv_b_pw1, conv_w_dw, conv_b_dw, conv_ln_g, conv_ln_b, conv_w_pw2, conv_b_pw2, kv_norm, w_k_shared, w_v_shared, diff_w_q, diff_lambda_q1, diff_lambda_k1, diff_lambda_q2, diff_lambda_k2, diff_subln_g, diff_w_o, mem_w_q, mem_w_k, mem_w_v, mem_w_o, ffn_w_gate, ffn_w_up, ffn_w_down):
    batch, seq, d = x.shape
    t = batch * seq
    row = lambda v: v.reshape(1, -1).astype(F32)
    bf = lambda w: w.astype(BF16)

    conv_w_pw1, conv_w_pw2, diff_w_q, diff_w_o = map(bf, (conv_w_pw1, conv_w_pw2, diff_w_q, diff_w_o))
    mem_w_q, mem_w_k, mem_w_v, mem_w_o = map(bf, (mem_w_q, mem_w_k, mem_w_v, mem_w_o))
    ffn_w_gate, ffn_w_up, ffn_w_down = map(bf, (ffn_w_gate, ffn_w_up, ffn_w_down))

    xs = x.reshape(t, d)
    tabs = _rope_tables(positions)
    mem_k, mem_v = _memkv(mem.reshape(-1, d), mem_w_k, mem_w_v)
    mem_k = mem_k.reshape(DEPTH, batch, -1, d)
    mem_v = mem_v.reshape(DEPTH, batch, -1, d)

    k_sh = v_sh = None
    for i in range(DEPTH):
        if i < N_A_LAYERS:
            u = _pw1(xs, row(norm_mix[i]), conv_w_pw1, row(conv_b_pw1[i]), i)
            wdw = jnp.repeat(conv_w_dw[i].astype(BF16), BF16_ROWS, axis=0)
            xs = _conv(u, xs, wdw, row(conv_b_dw[i]), row(conv_ln_g[i]), row(conv_ln_b[i]),
                       conv_w_pw2, row(conv_b_pw2[i]), i, batch, seq)
            xs = _memattn(xs, row(norm_mem[i]), mem_w_q, mem_k, mem_v, mem_w_o, i, batch, seq)
        else:
            b = i - N_A_LAYERS
            if b == 0:
                k_sh, v_sh = _shared_kv(xs, row(kv_norm), bf(w_k_shared), bf(w_v_shared), tabs)
            lambda_init = 0.8 - 0.6 * math.exp(-0.3 * i)
            q = _q_proj(xs, row(norm_mix[i]), diff_w_q, tabs, b)
            lam_rows = jnp.stack([diff_lambda_q1[b], diff_lambda_k1[b],
                                  diff_lambda_q2[b], diff_lambda_k2[b]]).astype(F32)
            lam_rows = jnp.pad(lam_rows, ((0, 4), (0, LANES - DIFF_HEAD_DIM)))
            attn_out = _diff_attn(q, k_sh, v_sh, lam_rows, row(diff_subln_g[b]), lambda_init,
                                  batch, seq)
            xs = _memattn(xs, row(norm_mem[i]), mem_w_q, mem_k, mem_v, mem_w_o, i, batch, seq,
                          attn_out, diff_w_o, b)
        xs = _ffn(xs, row(norm_ffn[i]), ffn_w_gate, ffn_w_up, ffn_w_down,
                  row(norm_final), final_norm=(i == DEPTH - 1), layer=i)
    return xs.reshape(batch, seq, d)
```
